```python
import jax, jax.numpy as jnp
from jax import lax
import numpy as np

D_MODEL = 2048
BATCH = 8
SEQ = 2048
DEPTH = 2

GLA_HEADS = 4
GLA_DK = D_MODEL // 8
GLA_DV = D_MODEL // 4
GLA_KEY_W = GLA_HEADS * GLA_DK
GLA_VAL_W = GLA_HEADS * GLA_DV
GLA_GATE_RANK = 16
GLA_GATE_TAU = 16.0
GLA_CHUNK = 64
CONV_CH = D_MODEL
CONV_WIDTH = 31
FFN_HIDDEN = (8 * D_MODEL + 3 * 256 - 1) // (3 * 256) * 256
RMS_EPS = 1e-6
LN_EPS = 1e-5
IN_SPLITS = (GLA_KEY_W, GLA_KEY_W, GLA_VAL_W, GLA_VAL_W, GLA_GATE_RANK, 2 * CONV_CH, 2 * D_MODEL)
IN_W = sum(IN_SPLITS)

kernel_name = 'hybrid_gla_conformer_conv_gated_merge'


def _rmsnorm(x, g):
    x32 = x.astype(jnp.float32)
    y = x32 * lax.rsqrt(jnp.mean(x32 * x32, axis=-1, keepdims=True) + RMS_EPS)
    return (y * g.astype(jnp.float32)).astype(x.dtype)


def _layernorm(x, g, b):
    x32 = x.astype(jnp.float32)
    mu = jnp.mean(x32, axis=-1, keepdims=True)
    xc = x32 - mu
    y = xc * lax.rsqrt(jnp.mean(xc * xc, axis=-1, keepdims=True) + LN_EPS)
    return (y * g.astype(jnp.float32) + b.astype(jnp.float32)).astype(x.dtype)


def _split_cols(t, sizes):
    idx, acc = [], 0
    for s in sizes[:-1]:
        acc += s
        idx.append(acc)
    return jnp.split(t, idx, axis=-1)


def _gla_chunked(q, k, v, log_a):
    B, S, H, DK = q.shape
    DV = v.shape[-1]
    N = S // GLA_CHUNK

    def to_chunks(t):
        return t.astype(jnp.float32).reshape(B, N, GLA_CHUNK, H, t.shape[-1]).transpose(1, 0, 3, 2, 4)

    qc = to_chunks(q) * (GLA_DK ** -0.5)
    kc = to_chunks(k)
    vc = to_chunks(v)
    bc = jnp.cumsum(to_chunks(log_a), axis=3)
    causal = jnp.tril(jnp.ones((GLA_CHUNK, GLA_CHUNK), dtype=bool))

    def step(state, inp):
        q_c, k_c, v_c, b_c = inp
        qe = q_c * jnp.exp(b_c)
        ke = k_c * jnp.exp(-b_c)
        att = jnp.where(causal, jnp.einsum('bhcd,bhsd->bhcs', qe, ke), 0.0)
        o = jnp.einsum('bhcs,bhse->bhce', att, v_c) + jnp.einsum('bhcd,bhde->bhce', qe, state)
        b_last = b_c[:, :, -1:, :]
        kd = k_c * jnp.exp(b_last - b_c)
        new_state = jnp.exp(b_last[:, :, 0, :])[..., None] * state + jnp.einsum('bhcd,bhce->bhde', kd, v_c)
        return new_state, o

    s0 = jnp.zeros((B, H, DK, DV), jnp.float32)
    _, o = lax.scan(step, s0, (qc, kc, vc, bc))
    return o.transpose(1, 0, 3, 2, 4).reshape(B, S, H, DV)


def _mixer(h, w_in, w_alpha2, b_alpha2, gla_norm, w_o_gla, conv_w, conv_b,
           conv_norm_g, conv_norm_b, w_pw2, b_pw2, w_out):
    B, S, _ = h.shape
    proj = h @ w_in
    q, k, v, r, a_lr, glu_in, gates = _split_cols(proj, IN_SPLITS)

    z = (a_lr @ w_alpha2 + b_alpha2).astype(jnp.float32)
    log_a = jax.nn.log_sigmoid(z) / GLA_GATE_TAU
    o = _gla_chunked(q.reshape(B, S, GLA_HEADS, GLA_DK),
                     k.reshape(B, S, GLA_HEADS, GLA_DK),
                     v.reshape(B, S, GLA_HEADS, GLA_DV),
                     log_a.reshape(B, S, GLA_HEADS, GLA_DK))
    o = o * lax.rsqrt(jnp.mean(o * o, axis=-1, keepdims=True) + RMS_EPS)
    o = (o.reshape(B, S, GLA_VAL_W) * gla_norm.astype(jnp.float32)).astype(h.dtype)
    branch_a = (o * jax.nn.silu(r)) @ w_o_gla

    u_val, u_gate = jnp.split(glu_in, 2, axis=-1)
    u = u_val * jax.nn.sigmoid(u_gate)
    u = lax.conv_general_dilated(u, conv_w[:, None, :], window_strides=(1,),
                                 padding=[(CONV_WIDTH - 1, 0)],
                                 dimension_numbers=('NWC', 'WIO', 'NWC'),
                                 feature_group_count=CONV_CH) + conv_b
    u = jax.nn.silu(_layernorm(u, conv_norm_g, conv_norm_b))
    branch_b = u @ w_pw2 + b_pw2

    g_a, g_b = jnp.split(gates, 2, axis=-1)
    merged = jax.nn.sigmoid(g_a) * branch_a + jax.nn.sigmoid(g_b) * branch_b
    return merged @ w_out


def _swiglu(h, w_gate, w_up, w_down):
    return (jax.nn.silu(h @ w_gate) * (h @ w_up)) @ w_down


def setup_inputs(seed: int = 0) -> dict:
    key = jax.random.key(seed)
    ks = jax.random.split(key, 19)
    L = DEPTH

    def nrm(k, shape, scale):
        return jax.random.normal(k, shape, jnp.float32) * scale

    return {
        'x': nrm(ks[0], (BATCH, SEQ, D_MODEL), 1.0),
        'norm_mix': 1.0 + nrm(ks[1], (L, D_MODEL), 0.02),
        'w_in': nrm(ks[2], (L, D_MODEL, IN_W), D_MODEL ** -0.5),
        'w_alpha2': nrm(ks[3], (L, GLA_GATE_RANK, GLA_KEY_W), GLA_GATE_RANK ** -0.5),
        'b_alpha2': nrm(ks[4], (L, GLA_KEY_W), 0.02),
        'gla_norm': 1.0 + nrm(ks[5], (L, GLA_VAL_W), 0.02),
        'w_o_gla': nrm(ks[6], (L, GLA_VAL_W, D_MODEL), GLA_VAL_W ** -0.5),
        'conv_w': nrm(ks[7], (L, CONV_WIDTH, CONV_CH), CONV_WIDTH ** -0.5),
        'conv_b': nrm(ks[8], (L, CONV_CH), 0.02),
        'conv_norm_g': 1.0 + nrm(ks[9], (L, CONV_CH), 0.02),
        'conv_norm_b': nrm(ks[10], (L, CONV_CH), 0.02),
        'w_pw2': nrm(ks[11], (L, CONV_CH, D_MODEL), CONV_CH ** -0.5),
        'b_pw2': nrm(ks[12], (L, D_MODEL), 0.02),
        'w_out': nrm(ks[13], (L, D_MODEL, D_MODEL), D_MODEL ** -0.5),
        'norm_ffn': 1.0 + nrm(ks[14], (L, D_MODEL), 0.02),
        'w_gate': nrm(ks[15], (L, D_MODEL, FFN_HIDDEN), D_MODEL ** -0.5),
        'w_up': nrm(ks[16], (L, D_MODEL, FFN_HIDDEN), D_MODEL ** -0.5),
        'w_down': nrm(ks[17], (L, FFN_HIDDEN, D_MODEL), FFN_HIDDEN ** -0.5),
        'norm_final': 1.0 + nrm(ks[18], (D_MODEL,), 0.02),
    }


def reference(x, norm_mix, w_in, w_alpha2, b_alpha2, gla_norm, w_o_gla, conv_w, conv_b,
              conv_norm_g, conv_norm_b, w_pw2, b_pw2, w_out, norm_ffn, w_gate, w_up, w_down,
              norm_final):
    h = x
    for l in range(DEPTH):
        h = h + _mixer(_rmsnorm(h, norm_mix[l]), w_in[l], w_alpha2[l], b_alpha2[l], gla_norm[l],
                       w_o_gla[l], conv_w[l], conv_b[l], conv_norm_g[l], conv_norm_b[l],
                       w_pw2[l], b_pw2[l], w_out[l])
        h = h + _swiglu(_rmsnorm(h, norm_ffn[l]), w_gate[l], w_up[l], w_down[l])
    return _rmsnorm(h, norm_final)
```

```python
import functools

import jax
import jax.numpy as jnp
from jax import lax
from jax.experimental import pallas as pl
from jax.experimental.pallas import tpu as pltpu

F32 = jnp.float32
BF16 = jnp.bfloat16

RMS_EPS = 1e-6
LN_EPS = 1e-5
GLA_HEADS = 4
GLA_CHUNK = 64
GLA_GATE_TAU = 16.0

LANES = 128
SUBLANES = 8
VMEM_LIMIT_BYTES = 56 * 1024 * 1024


def _params(*sem):
    return pltpu.CompilerParams(dimension_semantics=sem, vmem_limit_bytes=VMEM_LIMIT_BYTES)


def _tile(n, pref):
    t = min(n, pref)
    while n % t:
        t //= 2
    return t


def _dot(a, b):
    return jnp.dot(a, b, preferred_element_type=F32)


def _rmsnorm_rows(h_ref, g_ref, out_ref, row_chunk):
    g = g_ref[...]

    def body(i, carry):
        rows = pl.ds(pl.multiple_of(i * row_chunk, row_chunk), row_chunk)
        x = h_ref[rows, :]
        ms = jnp.mean(x * x, axis=-1, keepdims=True)
        out_ref[rows, :] = (x * lax.rsqrt(ms + RMS_EPS) * g).astype(out_ref.dtype)
        return carry

    lax.fori_loop(0, h_ref.shape[0] // row_chunk, body, 0)


def _in_proj_body(h_ref, g_ref, w_ref, walr_ref, proj_ref, alr_ref, xn_ref, *, row_chunk):
    @pl.when(pl.program_id(1) == 0)
    def _():
        _rmsnorm_rows(h_ref, g_ref, xn_ref, row_chunk)
        alr_ref[...] = _dot(xn_ref[...], walr_ref[...])

    proj_ref[...] = _dot(xn_ref[...], w_ref[...]).astype(proj_ref.dtype)


def _in_proj(h, g, w_main, w_alr):
    T, D = h.shape
    NW = w_main.shape[1]
    RP = w_alr.shape[1]
    tm, tn = _tile(T, 1024), _tile(NW, 1024)
    return pl.pallas_call(
        functools.partial(_in_proj_body, row_chunk=_tile(tm, 64)),
        grid=(T // tm, NW // tn),
        in_specs=[
            pl.BlockSpec((tm, D), lambda i, j: (i, 0)),
            pl.BlockSpec((1, D), lambda i, j: (0, 0)),
            pl.BlockSpec((D, tn), lambda i, j: (0, j)),
            pl.BlockSpec((D, RP), lambda i, j: (0, 0)),
        ],
        out_specs=[
            pl.BlockSpec((tm, tn), lambda i, j: (i, j)),
            pl.BlockSpec((tm, RP), lambda i, j: (i, 0)),
        ],
        out_shape=[jax.ShapeDtypeStruct((T, NW), BF16), jax.ShapeDtypeStruct((T, RP), F32)],
        scratch_shapes=[pltpu.VMEM((tm, D), BF16)],
        compiler_params=_params("parallel", "arbitrary"),
        name="in_proj",
    )(h, g, w_main, w_alr)


def _log_sigmoid(z):
    return jnp.minimum(z, 0.0) - jnp.log1p(jnp.exp(-jnp.abs(z)))


def _cumsum_rows(tril, x):
    hi = x.astype(BF16)
    r1 = x - hi.astype(F32)
    mid = r1.astype(BF16)
    lo = (r1 - mid.astype(F32)).astype(BF16)
    return _dot(tril, hi) + _dot(tril, mid) + _dot(tril, lo)


def _gla_body(q_ref, k_ref, v_ref, r_ref, alr_ref, wa_ref, ba_ref, gn_ref, o_ref, state_ref,
              *, n_chunks, scale):
    C = GLA_CHUNK
    DK, DV = state_ref.shape

    @pl.when(pl.program_id(2) == 0)
    def _():
        state_ref[...] = jnp.zeros_like(state_ref)

    row = lax.broadcasted_iota(jnp.int32, (C, C), 0)
    col = lax.broadcasted_iota(jnp.int32, (C, C), 1)
    causal = row >= col
    tril = causal.astype(BF16)
    lane_blk = min(LANES, DV)

    for c in range(n_chunks):
        rows = slice(c * C, (c + 1) * C)
        z = _dot(alr_ref[rows, :].astype(BF16), wa_ref[...]) + ba_ref[...]
        log_a = _log_sigmoid(z) / GLA_GATE_TAU
        b = _cumsum_rows(tril, log_a)
        b_last = b[C - 1:C, :]
        q = q_ref[rows, :].astype(F32) * scale
        k = k_ref[rows, :].astype(F32)
        v = v_ref[rows, :]
        qe = (q * jnp.exp(b)).astype(BF16)
        ke = (k * jnp.exp(-b)).astype(BF16)
        kd = (k * jnp.exp(b_last - b)).astype(BF16)
        att = lax.dot_general(qe, ke, (((1,), (1,)), ((), ())), preferred_element_type=F32)
        att = jnp.where(causal, att, 0.0).astype(BF16)
        o = _dot(att, v) + _dot(qe, state_ref[...].astype(BF16))
        upd = lax.dot_general(kd, v, (((0,), (0,)), ((), ())), preferred_element_type=F32)
        g_col = jnp.broadcast_to(jnp.exp(b_last), (lane_blk, DK)).T
        for jb in range(DV // lane_blk):
            cols = slice(jb * lane_blk, (jb + 1) * lane_blk)
            state_ref[:, cols] = state_ref[:, cols] * g_col + upd[:, cols]
        ms = jnp.mean(o * o, axis=-1, keepdims=True)
        on = o * lax.rsqrt(ms + RMS_EPS) * gn_ref[...]
        r = r_ref[rows, :].astype(F32)
        o_ref[rows, :] = (on * (r * jax.nn.sigmoid(r))).astype(o_ref.dtype)


def _gla(proj3, alr3, wa, ba, gn, *, key_w, val_w):
    B, S, _ = proj3.shape
    H = GLA_HEADS
    DK, DV = key_w // H, val_w // H
    RP = alr3.shape[-1]
    ts = _tile(S, 256)
    assert ts % GLA_CHUNK == 0 and (2 * key_w) % DV == 0
    k_blk0 = key_w // DK
    v_blk0 = (2 * key_w) // DV
    r_blk0 = (2 * key_w + val_w) // DV
    return pl.pallas_call(
        functools.partial(_gla_body, n_chunks=ts // GLA_CHUNK, scale=DK ** -0.5),
        grid=(B, H, S // ts),
        in_specs=[
            pl.BlockSpec((None, ts, DK), lambda b, h, s: (b, s, h)),
            pl.BlockSpec((None, ts, DK), lambda b, h, s: (b, s, k_blk0 + h)),
            pl.BlockSpec((None, ts, DV), lambda b, h, s: (b, s, v_blk0 + h)),
            pl.BlockSpec((None, ts, DV), lambda b, h, s: (b, s, r_blk0 + h)),
            pl.BlockSpec((None, ts, RP), lambda b, h, s: (b, s, 0)),
            pl.BlockSpec((RP, DK), lambda b, h, s: (0, h)),
            pl.BlockSpec((1, DK), lambda b, h, s: (0, h)),
            pl.BlockSpec((1, DV), lambda b, h, s: (0, h)),
        ],
        out_specs=pl.BlockSpec((None, ts, DV), lambda b, h, s: (b, s, h)),
        out_shape=jax.ShapeDtypeStruct((B, S, val_w), BF16),
        scratch_shapes=[pltpu.VMEM((DK, DV), F32)],
        compiler_params=_params("parallel", "parallel", "arbitrary"),
        name="gla",
    )(proj3, proj3, proj3, proj3, alr3, wa, ba, gn)


def _conv_body(val_ref, gate_ref, w_ref, cb_ref, lg_ref, lb_ref, o_ref, uext_ref, y_ref,
               *, halo, row_blk):
    ts, CH = val_ref.shape
    W = w_ref.shape[0]
    s = pl.program_id(1)

    @pl.when(s == 0)
    def _():
        uext_ref[0:halo, :] = jnp.zeros((halo, CH), F32)

    @pl.when(s > 0)
    def _():
        uext_ref[0:halo, :] = uext_ref[ts:ts + halo, :]

    uext_ref[halo:halo + ts, :] = val_ref[...].astype(F32) * jax.nn.sigmoid(gate_ref[...].astype(F32))

    lane_blk = min(LANES, CH)
    first = halo - (W - 1)

    def conv_lanes(lb, carry):
        lanes = pl.ds(pl.multiple_of(lb * lane_blk, lane_blk), lane_blk)
        taps = [w_ref[j:j + 1, lanes] for j in range(W)]
        bias = cb_ref[:, lanes]
        for rb in range(ts // row_blk):
            acc = jnp.broadcast_to(bias, (row_blk, lane_blk))
            for j in range(W):
                acc = acc + taps[j] * uext_ref[pl.ds(first + rb * row_blk + j, row_blk), lanes]
            y_ref[rb * row_blk:(rb + 1) * row_blk, lanes] = acc
        return carry

    lax.fori_loop(0, CH // lane_blk, conv_lanes, 0)

    lg = lg_ref[...]
    lb_ = lb_ref[...]

    def ln_rows(i, carry):
        rows = pl.ds(pl.multiple_of(i * row_blk, row_blk), row_blk)
        x = y_ref[rows, :]
        mu = jnp.mean(x, axis=-1, keepdims=True)
        xc = x - mu
        var = jnp.mean(xc * xc, axis=-1, keepdims=True)
        y = xc * lax.rsqrt(var + LN_EPS) * lg + lb_
        o_ref[rows, :] = (y * jax.nn.sigmoid(y)).astype(o_ref.dtype)
        return carry

    lax.fori_loop(0, ts // row_blk, ln_rows, 0)


def _conv_branch(proj3, conv_w, conv_b, ln_g, ln_b, *, glu_off, ch):
    B, S, _ = proj3.shape
    W = conv_w.shape[0]
    halo = -(-(W - 1) // SUBLANES) * SUBLANES
    ts = _tile(S, 256)
    assert ts >= halo and glu_off % ch == 0
    blk0 = glu_off // ch
    return pl.pallas_call(
        functools.partial(_conv_body, halo=halo, row_blk=_tile(ts, 64)),
        grid=(B, S // ts),
        in_specs=[
            pl.BlockSpec((None, ts, ch), lambda b, s: (b, s, blk0)),
            pl.BlockSpec((None, ts, ch), lambda b, s: (b, s, blk0 + 1)),
            pl.BlockSpec((W, ch), lambda b, s: (0, 0)),
            pl.BlockSpec((1, ch), lambda b, s: (0, 0)),
            pl.BlockSpec((1, ch), lambda b, s: (0, 0)),
            pl.BlockSpec((1, ch), lambda b, s: (0, 0)),
        ],
        out_specs=pl.BlockSpec((None, ts, ch), lambda b, s: (b, s, 0)),
        out_shape=jax.ShapeDtypeStruct((B, S, ch), BF16),
        scratch_shapes=[pltpu.VMEM((halo + ts, ch), F32), pltpu.VMEM((ts, ch), F32)],
        compiler_params=_params("parallel", "arbitrary"),
        name="conv_branch",
    )(proj3, proj3, conv_w, conv_b, ln_g, ln_b)


def _merge_body(xa_ref, u_ref, ga_ref, gb_ref, woa_ref, wpw_ref, bpw_ref, o_ref):
    a = _dot(xa_ref[...], woa_ref[...])
    bb = _dot(u_ref[...], wpw_ref[...]) + bpw_ref[...]
    ga = jax.nn.sigmoid(ga_ref[...].astype(F32))
    gb = jax.nn.sigmoid(gb_ref[...].astype(F32))
    o_ref[...] = (ga * a + gb * bb).astype(o_ref.dtype)


def _merge(xa, u, proj, w_o_gla, w_pw2, b_pw2, *, gate_off):
    T, VW = xa.shape
    CH = u.shape[1]
    D = w_o_gla.shape[1]
    tm, tn = _tile(T, 512), _tile(D, 1024)
    assert gate_off % tn == 0
    ga0 = gate_off // tn
    gb0 = (gate_off + D) // tn
    return pl.pallas_call(
        _merge_body,
        grid=(D // tn, T // tm),
        in_specs=[
            pl.BlockSpec((tm, VW), lambda j, i: (i, 0)),
            pl.BlockSpec((tm, CH), lambda j, i: (i, 0)),
            pl.BlockSpec((tm, tn), lambda j, i: (i, ga0 + j)),
            pl.BlockSpec((tm, tn), lambda j, i: (i, gb0 + j)),
            pl.BlockSpec((VW, tn), lambda j, i: (0, j)),
            pl.BlockSpec((CH, tn), lambda j, i: (0, j)),
            pl.BlockSpec((1, tn), lambda j, i: (0, j)),
        ],
        out_specs=pl.BlockSpec((tm, tn), lambda j, i: (i, j)),
        out_shape=jax.ShapeDtypeStruct((T, D), BF16),
        compiler_params=_params("parallel", "parallel"),
        name="merge",
    )(xa, u, proj, proj, w_o_gla, w_pw2, b_pw2)


def _resid_matmul_body(x_ref, w_ref, h_ref, o_ref):
    o_ref[...] = h_ref[...] + _dot(x_ref[...], w_ref[...])


def _resid_matmul(x, w, h, *, tm_pref, name):
    T, K = x.shape
    D = w.shape[1]
    tm, tn = _tile(T, tm_pref), _tile(D, 1024)
    return pl.pallas_call(
        _resid_matmul_body,
        grid=(D // tn, T // tm),
        in_specs=[
            pl.BlockSpec((tm, K), lambda j, i: (i, 0)),
            pl.BlockSpec((K, tn), lambda j, i: (0, j)),
            pl.BlockSpec((tm, tn), lambda j, i: (i, j)),
        ],
        out_specs=pl.BlockSpec((tm, tn), lambda j, i: (i, j)),
        out_shape=jax.ShapeDtypeStruct((T, D), F32),
        compiler_params=_params("parallel", "parallel"),
        name=name,
    )(x, w, h)


def _ffn_up_body(h_ref, g_ref, wg_ref, wu_ref, o_ref, xn_ref, *, row_chunk):
    @pl.when(pl.program_id(1) == 0)
    def _():
        _rmsnorm_rows(h_ref, g_ref, xn_ref, row_chunk)

    xn = xn_ref[...]
    gate = _dot(xn, wg_ref[...])
    up = _dot(xn, wu_ref[...])
    o_ref[...] = (gate * jax.nn.sigmoid(gate) * up).astype(o_ref.dtype)


def _ffn_up(h, g, w_gate, w_up):
    T, D = h.shape
    F = w_gate.shape[1]
    tm, tn = _tile(T, 1024), _tile(F, 512)
    return pl.pallas_call(
        functools.partial(_ffn_up_body, row_chunk=_tile(tm, 64)),
        grid=(T // tm, F // tn),
        in_specs=[
            pl.BlockSpec((tm, D), lambda i, j: (i, 0)),
            pl.BlockSpec((1, D), lambda i, j: (0, 0)),
            pl.BlockSpec((D, tn), lambda i, j: (0, j)),
            pl.BlockSpec((D, tn), lambda i, j: (0, j)),
        ],
        out_specs=pl.BlockSpec((tm, tn), lambda i, j: (i, j)),
        out_shape=jax.ShapeDtypeStruct((T, F), BF16),
        scratch_shapes=[pltpu.VMEM((tm, D), BF16)],
        compiler_params=_params("parallel", "arbitrary"),
        name="ffn_up",
    )(h, g, w_gate, w_up)


def _final_norm_body(h_ref, g_ref, o_ref, *, row_chunk):
    _rmsnorm_rows(h_ref, g_ref, o_ref, row_chunk)


def _final_norm(h, g):
    T, D = h.shape
    tm = _tile(T, 512)
    return pl.pallas_call(
        functools.partial(_final_norm_body, row_chunk=_tile(tm, 64)),
        grid=(T // tm,),
        in_specs=[pl.BlockSpec((tm, D), lambda i: (i, 0)), pl.BlockSpec((1, D), lambda i: (0, 0))],
        out_specs=pl.BlockSpec((tm, D), lambda i: (i, 0)),
        out_shape=jax.ShapeDtypeStruct((T, D), F32),
        compiler_params=_params("parallel"),
        name="final_norm",
    )(h, g)


def kernel(x, norm_mix, w_in, w_alpha2, b_alpha2, gla_norm, w_o_gla, conv_w, conv_b, conv_norm_g,
           conv_norm_b, w_pw2, b_pw2, w_out, norm_ffn, w_gate, w_up, w_down, norm_final):
    B, S, D = x.shape
    L = w_in.shape[0]
    T = B * S
    rank, key_w = w_alpha2.shape[1], w_alpha2.shape[2]
    val_w = gla_norm.shape[1]
    ch = conv_w.shape[2]
    alr_off = 2 * key_w + 2 * val_w
    glu_off = alr_off
    gate_off = glu_off + 2 * ch
    rank_pad = -(-rank // LANES) * LANES

    h = x.reshape(T, D)
    for l in range(L):
        w_main = jnp.concatenate([w_in[l, :, :alr_off], w_in[l, :, alr_off + rank:]], axis=1).astype(BF16)
        w_alr = jnp.pad(w_in[l, :, alr_off:alr_off + rank], ((0, 0), (0, rank_pad - rank))).astype(BF16)
        wa = jnp.pad(w_alpha2[l], ((0, rank_pad - rank), (0, 0))).astype(BF16)

        proj, alr = _in_proj(h, norm_mix[l][None, :], w_main, w_alr)
        proj3 = proj.reshape(B, S, proj.shape[1])
        xa = _gla(proj3, alr.reshape(B, S, rank_pad), wa, b_alpha2[l][None, :], gla_norm[l][None, :],
                  key_w=key_w, val_w=val_w)
        u = _conv_branch(proj3, conv_w[l], conv_b[l][None, :], conv_norm_g[l][None, :],
                         conv_norm_b[l][None, :], glu_off=glu_off, ch=ch)
        merged = _merge(xa.reshape(T, val_w), u.reshape(T, ch), proj, w_o_gla[l].astype(BF16),
                        w_pw2[l].astype(BF16), b_pw2[l][None, :], gate_off=gate_off)
        h = _resid_matmul(merged, w_out[l].astype(BF16), h, tm_pref=1024, name="out_proj")
        act = _ffn_up(h, norm_ffn[l][None, :], w_gate[l].astype(BF16), w_up[l].astype(BF16))
        h = _resid_matmul(act, w_down[l].astype(BF16), h, tm_pref=512, name="ffn_down")
    out = _final_norm(h, norm_final[None, :])
    return out.reshape(B, S, D)
```

```python
import functools

import jax
import jax.numpy as jnp
from jax import lax
from jax.experimental import pallas as pl
from jax.experimental.pallas import tpu as pltpu

F32 = jnp.float32
BF16 = jnp.bfloat16

RMS_EPS = 1e-6
LN_EPS = 1e-5
GLA_HEADS = 4
GLA_CHUNK = 64
GLA_GATE_TAU = 16.0

LANES = 128
SUBLANES = 8
VMEM_LIMIT_BYTES = 56 * 1024 * 1024


def _params(*sem):
    return pltpu.CompilerParams(dimension_semantics=sem, vmem_limit_bytes=VMEM_LIMIT_BYTES)


def _tile(n, pref):
    t = min(n, pref)
    while n % t:
        t //= 2
    return t


def _dot(a, b):
    return jnp.dot(a, b, preferred_element_type=F32)


def _sigmoid(x):
    return 0.5 * jnp.tanh(0.5 * x) + 0.5


def _rmsnorm_rows(h_ref, g_ref, out_ref, row_chunk):
    g = g_ref[...]

    def body(i, carry):
        rows = pl.ds(pl.multiple_of(i * row_chunk, row_chunk), row_chunk)
        x = h_ref[rows, :]
        ms = jnp.mean(x * x, axis=-1, keepdims=True)
        out_ref[rows, :] = (x * lax.rsqrt(ms + RMS_EPS) * g).astype(out_ref.dtype)
        return carry

    lax.fori_loop(0, h_ref.shape[0] // row_chunk, body, 0, unroll=2)


def _in_proj_body(h_ref, g_ref, w_ref, walr_ref, proj_ref, alr_ref, xn_ref, *, row_chunk):
    @pl.when(pl.program_id(1) == 0)
    def _():
        _rmsnorm_rows(h_ref, g_ref, xn_ref, row_chunk)
        alr_ref[...] = _dot(xn_ref[...], walr_ref[...])

    proj_ref[...] = _dot(xn_ref[...], w_ref[...]).astype(proj_ref.dtype)


def _in_proj(h, g, w_main, w_alr):
    T, D = h.shape
    NW = w_main.shape[1]
    RP = w_alr.shape[1]
    tm, tn = _tile(T, 1024), _tile(NW, 1024)
    return pl.pallas_call(
        functools.partial(_in_proj_body, row_chunk=_tile(tm, 64)),
        grid=(T // tm, NW // tn),
        in_specs=[
            pl.BlockSpec((tm, D), lambda i, j: (i, 0)),
            pl.BlockSpec((1, D), lambda i, j: (0, 0)),
            pl.BlockSpec((D, tn), lambda i, j: (0, j)),
            pl.BlockSpec((D, RP), lambda i, j: (0, 0)),
        ],
        out_specs=[
            pl.BlockSpec((tm, tn), lambda i, j: (i, j)),
            pl.BlockSpec((tm, RP), lambda i, j: (i, 0)),
        ],
        out_shape=[jax.ShapeDtypeStruct((T, NW), BF16), jax.ShapeDtypeStruct((T, RP), F32)],
        scratch_shapes=[pltpu.VMEM((tm, D), BF16)],
        compiler_params=_params("parallel", "arbitrary"),
        name="in_proj",
    )(h, g, w_main, w_alr)


def _log_sigmoid(z):
    return jnp.minimum(z, 0.0) - jnp.log1p(jnp.exp(-jnp.abs(z)))


def _cumsum_rows(tril, x):
    hi = x.astype(BF16)
    r1 = x - hi.astype(F32)
    mid = r1.astype(BF16)
    lo = (r1 - mid.astype(F32)).astype(BF16)
    return _dot(tril, hi) + _dot(tril, mid) + _dot(tril, lo)


def _gla_body(q_ref, k_ref, v_ref, r_ref, alr_ref, wa_ref, ba_ref, gn_ref, o_ref, state_ref, oacc_ref,
              *, n_chunks, n_heads, scale):
    C = GLA_CHUNK
    _, DK, DV = state_ref.shape
    TS = n_chunks * C

    @pl.when(pl.program_id(2) == 0)
    def _():
        state_ref[...] = jnp.zeros_like(state_ref)

    row = lax.broadcasted_iota(jnp.int32, (TS, TS), 0)
    col = lax.broadcasted_iota(jnp.int32, (TS, TS), 1)
    chunk_shift = C.bit_length() - 1
    causal = (row >= col) & ((row >> chunk_shift) == (col >> chunk_shift))
    tril = causal.astype(BF16)
    alr = alr_ref[...].astype(BF16)
    lane_blk = min(LANES, DV)

    for hh in range(n_heads):
        kc = slice(hh * DK, (hh + 1) * DK)
        vc = slice(hh * DV, (hh + 1) * DV)
        z = _dot(alr, wa_ref[:, kc]) + ba_ref[:, kc]
        log_a = _log_sigmoid(z) / GLA_GATE_TAU
        b = _cumsum_rows(tril, log_a)
        b_last = [b[(c + 1) * C - 1:(c + 1) * C, :] for c in range(n_chunks)]
        b_last_rows = jnp.concatenate([jnp.broadcast_to(bl, (C, DK)) for bl in b_last], axis=0)
        q = q_ref[:, kc].astype(F32) * scale
        k = k_ref[:, kc].astype(F32)
        v = v_ref[:, vc]
        qe = (q * jnp.exp(b)).astype(BF16)
        ke = (k * jnp.exp(-b)).astype(BF16)
        kd = (k * jnp.exp(b_last_rows - b)).astype(BF16)
        att = lax.dot_general(qe, ke, (((1,), (1,)), ((), ())), preferred_element_type=F32)
        att = jnp.where(causal, att, 0.0).astype(BF16)
        oacc_ref[:, vc] = _dot(att, v)

        for c in range(n_chunks):
            rows = slice(c * C, (c + 1) * C)
            oacc_ref[rows, vc] += _dot(qe[rows, :], state_ref[hh].astype(BF16))
            upd = lax.dot_general(kd[rows, :], v[rows, :], (((0,), (0,)), ((), ())),
                                  preferred_element_type=F32)
            g_col = jnp.broadcast_to(jnp.exp(b_last[c]), (lane_blk, DK)).T
            for jb in range(DV // lane_blk):
                cols = slice(jb * lane_blk, (jb + 1) * lane_blk)
                state_ref[hh, :, cols] = state_ref[hh, :, cols] * g_col + upd[:, cols]

        o = oacc_ref[:, vc]
        ms = jnp.mean(o * o, axis=-1, keepdims=True)
        on = o * lax.rsqrt(ms + RMS_EPS) * gn_ref[:, vc]
        r = r_ref[:, vc].astype(F32)
        o_ref[:, vc] = (on * (r * _sigmoid(r))).astype(o_ref.dtype)


def _gla(proj3, alr3, wa, ba, gn, *, key_w, val_w):
    B, S, _ = proj3.shape
    H = GLA_HEADS
    DK, DV = key_w // H, val_w // H
    RP = alr3.shape[-1]
    ts = _tile(S, 256)
    hps = 2
    KB, VB = hps * DK, hps * DV
    assert ts % GLA_CHUNK == 0 and H % hps == 0 and (2 * key_w) % VB == 0
    k_blk0 = key_w // KB
    v_blk0 = (2 * key_w) // VB
    r_blk0 = (2 * key_w + val_w) // VB
    return pl.pallas_call(
        functools.partial(_gla_body, n_chunks=ts // GLA_CHUNK, n_heads=hps, scale=DK ** -0.5),
        grid=(B, H // hps, S // ts),
        in_specs=[
            pl.BlockSpec((None, ts, KB), lambda b, h, s: (b, s, h)),
            pl.BlockSpec((None, ts, KB), lambda b, h, s: (b, s, k_blk0 + h)),
            pl.BlockSpec((None, ts, VB), lambda b, h, s: (b, s, v_blk0 + h)),
            pl.BlockSpec((None, ts, VB), lambda b, h, s: (b, s, r_blk0 + h)),
            pl.BlockSpec((None, ts, RP), lambda b, h, s: (b, s, 0)),
            pl.BlockSpec((RP, KB), lambda b, h, s: (0, h)),
            pl.BlockSpec((1, KB), lambda b, h, s: (0, h)),
            pl.BlockSpec((1, VB), lambda b, h, s: (0, h)),
        ],
        out_specs=pl.BlockSpec((None, ts, VB), lambda b, h, s: (b, s, h)),
        out_shape=jax.ShapeDtypeStruct((B, S, val_w), BF16),
        scratch_shapes=[pltpu.VMEM((hps, DK, DV), F32), pltpu.VMEM((ts, VB), F32)],
        compiler_params=_params("parallel", "parallel", "arbitrary"),
        name="gla",
    )(proj3, proj3, proj3, proj3, alr3, wa, ba, gn)


def _conv_body(val_ref, gate_ref, w_ref, cb_ref, lg_ref, lb_ref, o_ref, uext_ref, sh_ref, y_ref,
               *, halo, row_blk, ln_blk):
    ts, CH = val_ref.shape
    W = w_ref.shape[0]
    n_ext = halo + ts
    step = pl.program_id(1)

    @pl.when(step == 0)
    def _():
        uext_ref[0:halo, :] = jnp.zeros((halo, CH), F32)

    @pl.when(step > 0)
    def _():
        uext_ref[0:halo, :] = uext_ref[ts:ts + halo, :]

    uext_ref[halo:halo + ts, :] = val_ref[...].astype(F32) * _sigmoid(gate_ref[...].astype(F32))

    lane_blk = min(LANES, CH)
    first = halo - (W - 1)

    def conv_lanes(lb, carry):
        lanes = pl.ds(pl.multiple_of(lb * lane_blk, lane_blk), lane_blk)
        x = uext_ref[:, lanes]
        for s in range(1, SUBLANES):
            sh_ref[s] = pltpu.roll(x, n_ext - s, axis=0)
        taps = [w_ref[j:j + 1, lanes] for j in range(W)]
        bias = cb_ref[:, lanes]
        for rb in range(ts // row_blk):
            acc = jnp.broadcast_to(bias, (row_blk, lane_blk))
            for j in range(W):
                s = (first + j) % SUBLANES
                base = rb * row_blk + (first + j) - s
                if s == 0:
                    src = uext_ref[pl.ds(base, row_blk), lanes]
                else:
                    src = sh_ref[s, pl.ds(base, row_blk), :]
                acc = acc + taps[j] * src
            y_ref[rb * row_blk:(rb + 1) * row_blk, lanes] = acc
        return carry

    lax.fori_loop(0, CH // lane_blk, conv_lanes, 0)

    lg = lg_ref[...]
    lb_ = lb_ref[...]

    def ln_rows(i, carry):
        rows = pl.ds(pl.multiple_of(i * ln_blk, ln_blk), ln_blk)
        x = y_ref[rows, :]
        mu = jnp.mean(x, axis=-1, keepdims=True)
        xc = x - mu
        var = jnp.mean(xc * xc, axis=-1, keepdims=True)
        y = xc * lax.rsqrt(var + LN_EPS) * lg + lb_
        o_ref[rows, :] = (y * _sigmoid(y)).astype(o_ref.dtype)
        return carry

    lax.fori_loop(0, ts // ln_blk, ln_rows, 0, unroll=True)


def _conv_branch(proj3, conv_w, conv_b, ln_g, ln_b, *, glu_off, ch):
    B, S, _ = proj3.shape
    W = conv_w.shape[0]
    halo = -(-(W - 1) // SUBLANES) * SUBLANES
    ts = _tile(S, 256)
    assert ts >= halo and glu_off % ch == 0
    blk0 = glu_off // ch
    return pl.pallas_call(
        functools.partial(_conv_body, halo=halo, row_blk=_tile(ts, 64), ln_blk=_tile(ts, 16)),
        grid=(B, S // ts),
        in_specs=[
            pl.BlockSpec((None, ts, ch), lambda b, s: (b, s, blk0)),
            pl.BlockSpec((None, ts, ch), lambda b, s: (b, s, blk0 + 1)),
            pl.BlockSpec((W, ch), lambda b, s: (0, 0)),
            pl.BlockSpec((1, ch), lambda b, s: (0, 0)),
            pl.BlockSpec((1, ch), lambda b, s: (0, 0)),
            pl.BlockSpec((1, ch), lambda b, s: (0, 0)),
        ],
        out_specs=pl.BlockSpec((None, ts, ch), lambda b, s: (b, s, 0)),
        out_shape=jax.ShapeDtypeStruct((B, S, ch), BF16),
        scratch_shapes=[pltpu.VMEM((halo + ts, ch), F32),
                        pltpu.VMEM((SUBLANES, halo + ts, min(LANES, ch)), F32),
                        pltpu.VMEM((ts, ch), F32)],
        compiler_params=_params("parallel", "arbitrary"),
        name="conv_branch",
    )(proj3, proj3, conv_w, conv_b, ln_g, ln_b)


def _merge_body(xa_ref, u_ref, ga_ref, gb_ref, woa_ref, wpw_ref, bpw_ref, o_ref):
    a = _dot(xa_ref[...], woa_ref[...])
    bb = _dot(u_ref[...], wpw_ref[...]) + bpw_ref[...]
    ga = _sigmoid(ga_ref[...].astype(F32))
    gb = _sigmoid(gb_ref[...].astype(F32))
    o_ref[...] = (ga * a + gb * bb).astype(o_ref.dtype)


def _merge(xa, u, proj, w_o_gla, w_pw2, b_pw2, *, gate_off):
    T, VW = xa.shape
    CH = u.shape[1]
    D = w_o_gla.shape[1]
    tm, tn = _tile(T, 512), _tile(D, 1024)
    assert gate_off % tn == 0
    ga0 = gate_off // tn
    gb0 = (gate_off + D) // tn
    return pl.pallas_call(
        _merge_body,
        grid=(D // tn, T // tm),
        in_specs=[
            pl.BlockSpec((tm, VW), lambda j, i: (i, 0)),
            pl.BlockSpec((tm, CH), lambda j, i: (i, 0)),
            pl.BlockSpec((tm, tn), lambda j, i: (i, ga0 + j)),
            pl.BlockSpec((tm, tn), lambda j, i: (i, gb0 + j)),
            pl.BlockSpec((VW, tn), lambda j, i: (0, j)),
            pl.BlockSpec((CH, tn), lambda j, i: (0, j)),
            pl.BlockSpec((1, tn), lambda j, i: (0, j)),
        ],
        out_specs=pl.BlockSpec((tm, tn), lambda j, i: (i, j)),
        out_shape=jax.ShapeDtypeStruct((T, D), BF16),
        compiler_params=_params("parallel", "parallel"),
        name="merge",
    )(xa, u, proj, proj, w_o_gla, w_pw2, b_pw2)


def _resid_matmul_body(x_ref, w_ref, h_ref, o_ref):
    o_ref[...] = h_ref[...] + _dot(x_ref[...], w_ref[...])


def _resid_matmul(x, w, h, *, tm_pref, name):
    T, K = x.shape
    D = w.shape[1]
    tm, tn = _tile(T, tm_pref), _tile(D, 1024)
    return pl.pallas_call(
        _resid_matmul_body,
        grid=(D // tn, T // tm),
        in_specs=[
            pl.BlockSpec((tm, K), lambda j, i: (i, 0)),
            pl.BlockSpec((K, tn), lambda j, i: (0, j)),
            pl.BlockSpec((tm, tn), lambda j, i: (i, j)),
        ],
        out_specs=pl.BlockSpec((tm, tn), lambda j, i: (i, j)),
        out_shape=jax.ShapeDtypeStruct((T, D), F32),
        compiler_params=_params("parallel", "parallel"),
        name=name,
    )(x, w, h)


def _ffn_up_body(h_ref, g_ref, wg_ref, wu_ref, o_ref, xn_ref, *, row_chunk):
    @pl.when(pl.program_id(1) == 0)
    def _():
        _rmsnorm_rows(h_ref, g_ref, xn_ref, row_chunk)

    xn = xn_ref[...]
    gate = _dot(xn, wg_ref[...])
    up = _dot(xn, wu_ref[...])
    o_ref[...] = (gate * _sigmoid(gate) * up).astype(o_ref.dtype)


def _ffn_up(h, g, w_gate, w_up):
    T, D = h.shape
    F = w_gate.shape[1]
    tm, tn = _tile(T, 1024), _tile(F, 512)
    return pl.pallas_call(
        functools.partial(_ffn_up_body, row_chunk=_tile(tm, 64)),
        grid=(T // tm, F // tn),
        in_specs=[
            pl.BlockSpec((tm, D), lambda i, j: (i, 0)),
            pl.BlockSpec((1, D), lambda i, j: (0, 0)),
            pl.BlockSpec((D, tn), lambda i, j: (0, j)),
            pl.BlockSpec((D, tn), lambda i, j: (0, j)),
        ],
        out_specs=pl.BlockSpec((tm, tn), lambda i, j: (i, j)),
        out_shape=jax.ShapeDtypeStruct((T, F), BF16),
        scratch_shapes=[pltpu.VMEM((tm, D), BF16)],
        compiler_params=_params("parallel", "arbitrary"),
        name="ffn_up",
    )(h, g, w_gate, w_up)


def _final_norm_body(h_ref, g_ref, o_ref, *, row_chunk):
    _rmsnorm_rows(h_ref, g_ref, o_ref, row_chunk)


def _final_norm(h, g):
    T, D = h.shape
    tm = _tile(T, 512)
    return pl.pallas_call(
        functools.partial(_final_norm_body, row_chunk=_tile(tm, 64)),
        grid=(T // tm,),
        in_specs=[pl.BlockSpec((tm, D), lambda i: (i, 0)), pl.BlockSpec((1, D), lambda i: (0, 0))],
        out_specs=pl.BlockSpec((tm, D), lambda i: (i, 0)),
        out_shape=jax.ShapeDtypeStruct((T, D), F32),
        compiler_params=_params("parallel"),
        name="final_norm",
    )(h, g)


def kernel(x, norm_mix, w_in, w_alpha2, b_alpha2, gla_norm, w_o_gla, conv_w, conv_b, conv_norm_g,
           conv_norm_b, w_pw2, b_pw2, w_out, norm_ffn, w_gate, w_up, w_down, norm_final):
    B, S, D = x.shape
    L = w_in.shape[0]
    T = B * S
    rank, key_w = w_alpha2.shape[1], w_alpha2.shape[2]
    val_w = gla_norm.shape[1]
    ch = conv_w.shape[2]
    alr_off = 2 * key_w + 2 * val_w
    glu_off = alr_off
    gate_off = glu_off + 2 * ch
    rank_pad = -(-rank // LANES) * LANES

    h = x.reshape(T, D)
    for l in range(L):
        w_main = jnp.concatenate([w_in[l, :, :alr_off], w_in[l, :, alr_off + rank:]], axis=1).astype(BF16)
        w_alr = jnp.pad(w_in[l, :, alr_off:alr_off + rank], ((0, 0), (0, rank_pad - rank))).astype(BF16)
        wa = jnp.pad(w_alpha2[l], ((0, rank_pad - rank), (0, 0))).astype(BF16)

        proj, alr = _in_proj(h, norm_mix[l][None, :], w_main, w_alr)
        proj3 = proj.reshape(B, S, proj.shape[1])
        xa = _gla(proj3, alr.reshape(B, S, rank_pad), wa, b_alpha2[l][None, :], gla_norm[l][None, :],
                  key_w=key_w, val_w=val_w)
        u = _conv_branch(proj3, conv_w[l], conv_b[l][None, :], conv_norm_g[l][None, :],
                         conv_norm_b[l][None, :], glu_off=glu_off, ch=ch)
        merged = _merge(xa.reshape(T, val_w), u.reshape(T, ch), proj, w_o_gla[l].astype(BF16),
                        w_pw2[l].astype(BF16), b_pw2[l][None, :], gate_off=gate_off)
        h = _resid_matmul(merged, w_out[l].astype(BF16), h, tm_pref=1024, name="out_proj")
        act = _ffn_up(h, norm_ffn[l][None, :], w_gate[l].astype(BF16), w_up[l].astype(BF16))
        h = _resid_matmul(act, w_down[l].astype(BF16), h, tm_pref=512, name="ffn_down")
    out = _final_norm(h, norm_final[None, :])
    return out.reshape(B, S, D)
```

```python
import functools

import jax
import jax.numpy as jnp
from jax import lax
from jax.experimental import pallas as pl
from jax.experimental.pallas import tpu as pltpu

F32 = jnp.float32
BF16 = jnp.bfloat16

RMS_EPS = 1e-6
LN_EPS = 1e-5
GLA_HEADS = 4
GLA_CHUNK = 64
GLA_GATE_TAU = 16.0

LANES = 128
SUBLANES = 8
VMEM_LIMIT_BYTES = 56 * 1024 * 1024


def _params(*sem):
    return pltpu.CompilerParams(dimension_semantics=sem, vmem_limit_bytes=VMEM_LIMIT_BYTES)


def _tile(n, pref):
    t = min(n, pref)
    while n % t:
        t //= 2
    return t


def _dot(a, b):
    return jnp.dot(a, b, preferred_element_type=F32)


def _dot_nt(a, b):
    return lax.dot_general(a, b, (((1,), (1,)), ((), ())), preferred_element_type=F32)


def _dot_tn(a, b):
    return lax.dot_general(a, b, (((0,), (0,)), ((), ())), preferred_element_type=F32)


def _sigmoid(x):
    return 0.5 * jnp.tanh(0.5 * x) + 0.5


def _rmsnorm_rows(h_ref, g_ref, out_ref, row_chunk):
    g = g_ref[...]

    def body(i, carry):
        rows = pl.ds(pl.multiple_of(i * row_chunk, row_chunk), row_chunk)
        x = h_ref[rows, :]
        ms = jnp.mean(x * x, axis=-1, keepdims=True)
        out_ref[rows, :] = (x * lax.rsqrt(ms + RMS_EPS) * g).astype(out_ref.dtype)
        return carry

    lax.fori_loop(0, h_ref.shape[0] // row_chunk, body, 0, unroll=2)


def _cast_rows(w_ref, out_ref, row_chunk):
    def body(i, carry):
        rows = pl.ds(pl.multiple_of(i * row_chunk, row_chunk), row_chunk)
        out_ref[rows, :] = w_ref[rows, :].astype(out_ref.dtype)
        return carry

    lax.fori_loop(0, w_ref.shape[0] // row_chunk, body, 0)


def _vec_spec(n, l, nidx):
    return pl.BlockSpec((None, 1, n), lambda *idx: (l, 0, 0))


def _norm_body(h_ref, g_ref, o_ref, *, row_chunk):
    _rmsnorm_rows(h_ref, g_ref, o_ref, row_chunk)


def _norm(h, gains, l, out_dtype, name):
    T, D = h.shape
    tm = _tile(T, 512)
    return pl.pallas_call(
        functools.partial(_norm_body, row_chunk=_tile(tm, 64)),
        grid=(T // tm,),
        in_specs=[pl.BlockSpec((tm, D), lambda i: (i, 0)), _vec_spec(D, l, 1)],
        out_specs=pl.BlockSpec((tm, D), lambda i: (i, 0)),
        out_shape=jax.ShapeDtypeStruct((T, D), out_dtype),
        compiler_params=_params("parallel"),
        name=name,
    )(h, gains)


def _mix_norm_body(h_ref, g_ref, walr_ref, xn_ref, alr_ref, walr_scr, *, row_chunk):
    @pl.when(pl.program_id(0) == 0)
    def _():
        walr_scr[...] = walr_ref[...].astype(BF16)

    _rmsnorm_rows(h_ref, g_ref, xn_ref, row_chunk)
    alr_ref[...] = _dot(xn_ref[...], walr_scr[...])


def _mix_norm(h, gains, w_in, l, *, alr_off):
    T, D = h.shape
    tm = _tile(T, 512)
    assert alr_off % LANES == 0
    return pl.pallas_call(
        functools.partial(_mix_norm_body, row_chunk=_tile(tm, 64)),
        grid=(T // tm,),
        in_specs=[
            pl.BlockSpec((tm, D), lambda i: (i, 0)),
            _vec_spec(D, l, 1),
            pl.BlockSpec((None, D, LANES), lambda i: (l, 0, alr_off // LANES)),
        ],
        out_specs=[pl.BlockSpec((tm, D), lambda i: (i, 0)), pl.BlockSpec((tm, LANES), lambda i: (i, 0))],
        out_shape=[jax.ShapeDtypeStruct((T, D), BF16), jax.ShapeDtypeStruct((T, LANES), F32)],
        scratch_shapes=[pltpu.VMEM((D, LANES), BF16)],
        compiler_params=_params("arbitrary"),
        name="mix_norm",
    )(h, gains, w_in)


def _in_proj_body(x_ref, wa_ref, wb_ref, o_ref, wscr_ref, *, shift_from, shift, row_chunk):
    j = pl.program_id(0)
    tn = wa_ref.shape[1]

    @pl.when((pl.program_id(1) == 0) & (j < shift_from))
    def _():
        _cast_rows(wa_ref, wscr_ref, row_chunk)

    @pl.when((pl.program_id(1) == 0) & (j >= shift_from))
    def _():
        keep = lax.broadcasted_iota(jnp.int32, (row_chunk, LANES), 1) < (LANES - shift)

        def body(i, carry):
            rows = pl.ds(pl.multiple_of(i * row_chunk, row_chunk), row_chunk)
            nxt = pltpu.roll(wb_ref[rows, :], LANES - shift, axis=1)
            for kb in reversed(range(tn // LANES)):
                cols = slice(kb * LANES, (kb + 1) * LANES)
                cur = pltpu.roll(wa_ref[rows, cols], LANES - shift, axis=1)
                wscr_ref[rows, cols] = jnp.where(keep, cur, nxt).astype(wscr_ref.dtype)
                nxt = cur
            return carry

        lax.fori_loop(0, wa_ref.shape[0] // row_chunk, body, 0)

    o_ref[...] = _dot(x_ref[...], wscr_ref[...]).astype(o_ref.dtype)


def _in_proj(xn, w_in, l, *, alr_off, rank):
    T, D = xn.shape
    NW = w_in.shape[2] - rank
    tm, tn = _tile(T, 1024), _tile(NW, 1024)
    assert alr_off % tn == 0 and tn % LANES == 0 and 0 < rank < LANES
    lpt = tn // LANES
    return pl.pallas_call(
        functools.partial(_in_proj_body, shift_from=alr_off // tn, shift=rank, row_chunk=_tile(D, 64)),
        grid=(NW // tn, T // tm),
        in_specs=[
            pl.BlockSpec((tm, D), lambda j, i: (i, 0)),
            pl.BlockSpec((None, D, tn), lambda j, i: (l, 0, j)),
            pl.BlockSpec((None, D, LANES), lambda j, i: (l, 0, (j + 1) * lpt)),
        ],
        out_specs=pl.BlockSpec((tm, tn), lambda j, i: (i, j)),
        out_shape=jax.ShapeDtypeStruct((T, NW), BF16),
        scratch_shapes=[pltpu.VMEM((D, tn), BF16)],
        compiler_params=_params("arbitrary", "arbitrary"),
        name="in_proj",
    )(xn, w_in, w_in)


def _log_sigmoid(z):
    return jnp.minimum(z, 0.0) - jnp.log1p(jnp.exp(-jnp.abs(z)))


def _cumsum_rows(tril, x):
    hi = x.astype(BF16)
    r1 = x - hi.astype(F32)
    mid = r1.astype(BF16)
    lo = (r1 - mid.astype(F32)).astype(BF16)
    return _dot(tril, hi) + _dot(tril, mid) + _dot(tril, lo)


def _gla_body(alr_ref, q_ref, k_ref, v_ref, r_ref, wa_ref, ba_ref, gn_ref, o_ref,
              state_ref, att_ref, *, n_chunks, n_heads, scale):
    C = GLA_CHUNK
    NC = n_chunks
    _, DK, DV = state_ref.shape
    TS = NC * C
    heads = range(n_heads)
    kcs = [slice(hh * DK, (hh + 1) * DK) for hh in heads]
    vcs = [slice(hh * DV, (hh + 1) * DV) for hh in heads]
    chunk = [slice(c * C, (c + 1) * C) for c in range(NC)]

    @pl.when(pl.program_id(2) == 0)
    def _():
        state_ref[...] = jnp.zeros_like(state_ref)

    row = lax.broadcasted_iota(jnp.int32, (TS, TS), 0)
    col = lax.broadcasted_iota(jnp.int32, (TS, TS), 1)
    chunk_shift = C.bit_length() - 1
    causal = (row >= col) & ((row >> chunk_shift) == (col >> chunk_shift))
    tril = causal.astype(BF16)

    z = _dot(alr_ref[...].astype(BF16), wa_ref[...]) + ba_ref[...]
    log_a = _log_sigmoid(z) / GLA_GATE_TAU
    b = _cumsum_rows(tril, log_a)
    bl = [b[(c + 1) * C - 1:(c + 1) * C, :] for c in range(NC)]
    E = [jnp.zeros_like(bl[0])]
    for c in range(NC):
        E.append(E[c] + bl[c])
    bl_rows = jnp.concatenate([jnp.broadcast_to(x, (C, x.shape[1])) for x in bl], axis=0)
    qf = q_ref[...].astype(F32) * scale * jnp.exp(b)
    kf = k_ref[...].astype(F32)
    kdf = kf * jnp.exp(bl_rows - b)
    qe = qf.astype(BF16)
    ke = (kf * jnp.exp(-b)).astype(BF16)
    kd = kdf.astype(BF16)
    cross_lhs = []
    for j in range(NC - 1):
        parts = [qe[chunk[j + 1], :]]
        parts += [(qf[chunk[c], :] * jnp.exp(E[c] - E[j + 1])).astype(BF16) for c in range(j + 2, NC)]
        cross_lhs.append(jnp.concatenate(parts, axis=0))
    qs = jnp.concatenate(
        [qe[chunk[0], :]] + [(qf[chunk[c], :] * jnp.exp(E[c])).astype(BF16) for c in range(1, NC)], axis=0)
    kds = jnp.concatenate(
        [(kdf[chunk[c], :] * jnp.exp(E[NC] - E[c + 1])).astype(BF16) for c in range(NC - 1)]
        + [kd[chunk[NC - 1], :]], axis=0)
    g_end = jnp.exp(E[NC])

    for hh in heads:
        att_ref[hh] = jnp.where(causal, _dot_nt(qe[:, kcs[hh]], ke[:, kcs[hh]]), 0.0)
    for j in range(NC - 1):
        for hh in heads:
            att_ref[hh, (j + 1) * C:TS, chunk[j]] = _dot_nt(cross_lhs[j][:, kcs[hh]], kd[chunk[j], kcs[hh]])
    o = []
    for hh in heads:
        lhs_o = jnp.concatenate([att_ref[hh].astype(BF16), qs[:, kcs[hh]]], axis=1)
        rhs_o = jnp.concatenate([v_ref[:, vcs[hh]], state_ref[hh].astype(BF16)], axis=0)
        o.append(_dot(lhs_o, rhs_o))
    lane_blk = min(LANES, DV)
    for hh in heads:
        upd = _dot_tn(kds[:, kcs[hh]], v_ref[:, vcs[hh]])
        g_col = jnp.broadcast_to(g_end[:, kcs[hh]], (lane_blk, DK)).T
        for jb in range(DV // lane_blk):
            cols = slice(jb * lane_blk, (jb + 1) * lane_blk)
            state_ref[hh, :, cols] = state_ref[hh, :, cols] * g_col + upd[:, cols]
    for hh in heads:
        ms = jnp.mean(o[hh] * o[hh], axis=-1, keepdims=True)
        on = o[hh] * lax.rsqrt(ms + RMS_EPS) * gn_ref[:, vcs[hh]]
        r = r_ref[:, vcs[hh]].astype(F32)
        o_ref[:, vcs[hh]] = (on * (r * _sigmoid(r))).astype(o_ref.dtype)


def _gla(alr3, proj3, wa, ba, gn, l, *, key_w, val_w):
    B, S, RP = alr3.shape
    H = GLA_HEADS
    DK, DV = key_w // H, val_w // H
    ts = _tile(S, 256)
    hps = 2
    KB, VB = hps * DK, hps * DV
    assert ts % GLA_CHUNK == 0 and H % hps == 0 and (2 * key_w) % VB == 0 and RP == wa.shape[1]
    k_blk0 = key_w // KB
    v_blk0 = (2 * key_w) // VB
    r_blk0 = (2 * key_w + val_w) // VB
    return pl.pallas_call(
        functools.partial(_gla_body, n_chunks=ts // GLA_CHUNK, n_heads=hps, scale=DK ** -0.5),
        grid=(B, H // hps, S // ts),
        in_specs=[
            pl.BlockSpec((None, ts, RP), lambda b, h, s: (b, s, 0)),
            pl.BlockSpec((None, ts, KB), lambda b, h, s: (b, s, h)),
            pl.BlockSpec((None, ts, KB), lambda b, h, s: (b, s, k_blk0 + h)),
            pl.BlockSpec((None, ts, VB), lambda b, h, s: (b, s, v_blk0 + h)),
            pl.BlockSpec((None, ts, VB), lambda b, h, s: (b, s, r_blk0 + h)),
            pl.BlockSpec((None, RP, KB), lambda b, h, s: (l, 0, h)),
            pl.BlockSpec((None, 1, KB), lambda b, h, s: (l, 0, h)),
            pl.BlockSpec((None, 1, VB), lambda b, h, s: (l, 0, h)),
        ],
        out_specs=pl.BlockSpec((None, ts, VB), lambda b, h, s: (b, s, h)),
        out_shape=jax.ShapeDtypeStruct((B, S, val_w), BF16),
        scratch_shapes=[pltpu.VMEM((hps, DK, DV), F32), pltpu.VMEM((hps, ts, ts), F32)],
        compiler_params=_params("parallel", "parallel", "arbitrary"),
        name="gla",
    )(alr3, proj3, proj3, proj3, proj3, wa, ba, gn)


def _conv_body(val_ref, gate_ref, w_ref, cb_ref, lg_ref, lb_ref, o_ref, uext_ref, sh_ref, y_ref,
               *, halo, row_blk, ln_blk):
    ts, CH = val_ref.shape
    W = w_ref.shape[0]
    n_ext = halo + ts
    step = pl.program_id(1)

    @pl.when(step == 0)
    def _():
        uext_ref[0:halo, :] = jnp.zeros((halo, CH), F32)

    @pl.when(step > 0)
    def _():
        uext_ref[0:halo, :] = uext_ref[ts:ts + halo, :]

    uext_ref[halo:halo + ts, :] = val_ref[...].astype(F32) * _sigmoid(gate_ref[...].astype(F32))

    lane_blk = min(LANES, CH)
    first = halo - (W - 1)

    def conv_lanes(lb, carry):
        lanes = pl.ds(pl.multiple_of(lb * lane_blk, lane_blk), lane_blk)
        x = uext_ref[:, lanes]
        for s in range(1, SUBLANES):
            sh_ref[s] = pltpu.roll(x, n_ext - s, axis=0)
        taps = [w_ref[j:j + 1, lanes] for j in range(W)]
        bias = cb_ref[:, lanes]
        for rb in range(ts // row_blk):
            acc = jnp.broadcast_to(bias, (row_blk, lane_blk))
            for j in range(W):
                s = (first + j) % SUBLANES
                base = rb * row_blk + (first + j) - s
                if s == 0:
                    src = uext_ref[pl.ds(base, row_blk), lanes]
                else:
                    src = sh_ref[s, pl.ds(base, row_blk), :]
                acc = acc + taps[j] * src
            y_ref[rb * row_blk:(rb + 1) * row_blk, lanes] = acc
        return carry

    lax.fori_loop(0, CH // lane_blk, conv_lanes, 0)

    lg = lg_ref[...]
    lb_ = lb_ref[...]

    def ln_rows(i, carry):
        rows = pl.ds(pl.multiple_of(i * ln_blk, ln_blk), ln_blk)
        x = y_ref[rows, :]
        mu = jnp.mean(x, axis=-1, keepdims=True)
        xc = x - mu
        var = jnp.mean(xc * xc, axis=-1, keepdims=True)
        y = xc * lax.rsqrt(var + LN_EPS) * lg + lb_
        o_ref[rows, :] = (y * _sigmoid(y)).astype(o_ref.dtype)
        return carry

    lax.fori_loop(0, ts // ln_blk, ln_rows, 0, unroll=True)


def _conv_branch(proj3, conv_w, conv_b, ln_g, ln_b, l, *, glu_off, ch):
    B, S, _ = proj3.shape
    W = conv_w.shape[1]
    halo = -(-(W - 1) // SUBLANES) * SUBLANES
    ts = _tile(S, 256)
    assert ts >= halo and glu_off % ch == 0
    blk0 = glu_off // ch
    return pl.pallas_call(
        functools.partial(_conv_body, halo=halo, row_blk=_tile(ts, 64), ln_blk=_tile(ts, 16)),
        grid=(B, S // ts),
        in_specs=[
            pl.BlockSpec((None, ts, ch), lambda b, s: (b, s, blk0)),
            pl.BlockSpec((None, ts, ch), lambda b, s: (b, s, blk0 + 1)),
            pl.BlockSpec((None, W, ch), lambda b, s: (l, 0, 0)),
            _vec_spec(ch, l, 2), _vec_spec(ch, l, 2), _vec_spec(ch, l, 2),
        ],
        out_specs=pl.BlockSpec((None, ts, ch), lambda b, s: (b, s, 0)),
        out_shape=jax.ShapeDtypeStruct((B, S, ch), BF16),
        scratch_shapes=[pltpu.VMEM((halo + ts, ch), F32),
                        pltpu.VMEM((SUBLANES, halo + ts, min(LANES, ch)), F32),
                        pltpu.VMEM((ts, ch), F32)],
        compiler_params=_params("parallel", "arbitrary"),
        name="conv_branch",
    )(proj3, proj3, conv_w, conv_b, ln_g, ln_b)


def _merge_body(xa_ref, u_ref, ga_ref, gb_ref, woa_ref, wpw_ref, bpw_ref, o_ref, woa_scr, wpw_scr,
                *, row_chunk):
    @pl.when(pl.program_id(1) == 0)
    def _():
        _cast_rows(woa_ref, woa_scr, row_chunk)
        _cast_rows(wpw_ref, wpw_scr, row_chunk)

    a = _dot(xa_ref[...], woa_scr[...])
    bb = _dot(u_ref[...], wpw_scr[...]) + bpw_ref[...]
    ga = _sigmoid(ga_ref[...].astype(F32))
    gb = _sigmoid(gb_ref[...].astype(F32))
    o_ref[...] = (ga * a + gb * bb).astype(o_ref.dtype)


def _merge(xa, u, proj, w_o_gla, w_pw2, b_pw2, l, *, gate_off):
    T, VW = xa.shape
    CH = u.shape[1]
    D = w_o_gla.shape[2]
    tm, tn = _tile(T, 512), _tile(D, 512)
    assert gate_off % tn == 0
    ga0 = gate_off // tn
    gb0 = (gate_off + D) // tn
    return pl.pallas_call(
        functools.partial(_merge_body, row_chunk=_tile(min(VW, CH), 256)),
        grid=(D // tn, T // tm),
        in_specs=[
            pl.BlockSpec((tm, VW), lambda j, i: (i, 0)),
            pl.BlockSpec((tm, CH), lambda j, i: (i, 0)),
            pl.BlockSpec((tm, tn), lambda j, i: (i, ga0 + j)),
            pl.BlockSpec((tm, tn), lambda j, i: (i, gb0 + j)),
            pl.BlockSpec((None, VW, tn), lambda j, i: (l, 0, j)),
            pl.BlockSpec((None, CH, tn), lambda j, i: (l, 0, j)),
            pl.BlockSpec((None, 1, tn), lambda j, i: (l, 0, j)),
        ],
        out_specs=pl.BlockSpec((tm, tn), lambda j, i: (i, j)),
        out_shape=jax.ShapeDtypeStruct((T, D), BF16),
        scratch_shapes=[pltpu.VMEM((VW, tn), BF16), pltpu.VMEM((CH, tn), BF16)],
        compiler_params=_params("arbitrary", "arbitrary"),
        name="merge",
    )(xa, u, proj, proj, w_o_gla, w_pw2, b_pw2)


def _out_proj_body(x_ref, w_ref, h_ref, g_ref, ho_ref, xn_ref, wscr_ref, *, row_chunk):
    @pl.when(pl.program_id(0) == 0)
    def _():
        _cast_rows(w_ref, wscr_ref, row_chunk)

    ho_ref[...] = h_ref[...] + _dot(x_ref[...], wscr_ref[...])
    _rmsnorm_rows(ho_ref, g_ref, xn_ref, _tile(ho_ref.shape[0], 64))


def _out_proj(x, w_out, h, gains, l):
    T, K = x.shape
    D = w_out.shape[2]
    tm = _tile(T, 512)
    return pl.pallas_call(
        functools.partial(_out_proj_body, row_chunk=_tile(K, 256)),
        grid=(T // tm,),
        in_specs=[
            pl.BlockSpec((tm, K), lambda i: (i, 0)),
            pl.BlockSpec((None, K, D), lambda i: (l, 0, 0), pipeline_mode=pl.Buffered(1)),
            pl.BlockSpec((tm, D), lambda i: (i, 0)),
            _vec_spec(D, l, 1),
        ],
        out_specs=[pl.BlockSpec((tm, D), lambda i: (i, 0)), pl.BlockSpec((tm, D), lambda i: (i, 0))],
        out_shape=[jax.ShapeDtypeStruct((T, D), F32), jax.ShapeDtypeStruct((T, D), BF16)],
        scratch_shapes=[pltpu.VMEM((K, D), BF16)],
        compiler_params=_params("arbitrary"),
        name="out_proj",
    )(x, w_out, h, gains)


def _ffn_up_body(x_ref, wg_ref, wu_ref, o_ref, wg_scr, wu_scr, *, row_chunk):
    @pl.when(pl.program_id(1) == 0)
    def _():
        _cast_rows(wg_ref, wg_scr, row_chunk)
        _cast_rows(wu_ref, wu_scr, row_chunk)

    xn = x_ref[...]
    gate = _dot(xn, wg_scr[...])
    up = _dot(xn, wu_scr[...])
    o_ref[...] = (gate * _sigmoid(gate) * up).astype(o_ref.dtype)


def _ffn_up(xn, w_gate, w_up, l):
    T, D = xn.shape
    F = w_gate.shape[2]
    tm, tn = _tile(T, 1024), _tile(F, 512)
    return pl.pallas_call(
        functools.partial(_ffn_up_body, row_chunk=_tile(D, 256)),
        grid=(F // tn, T // tm),
        in_specs=[
            pl.BlockSpec((tm, D), lambda j, i: (i, 0)),
            pl.BlockSpec((None, D, tn), lambda j, i: (l, 0, j)),
            pl.BlockSpec((None, D, tn), lambda j, i: (l, 0, j)),
        ],
        out_specs=pl.BlockSpec((tm, tn), lambda j, i: (i, j)),
        out_shape=jax.ShapeDtypeStruct((T, F), BF16),
        scratch_shapes=[pltpu.VMEM((D, tn), BF16), pltpu.VMEM((D, tn), BF16)],
        compiler_params=_params("arbitrary", "arbitrary"),
        name="ffn_up",
    )(xn, w_gate, w_up)


def _ffn_down_body(x_ref, w_ref, h_ref, o_ref, wscr_ref, *, row_chunk):
    @pl.when(pl.program_id(1) == 0)
    def _():
        _cast_rows(w_ref, wscr_ref, row_chunk)

    o_ref[...] = h_ref[...] + _dot(x_ref[...], wscr_ref[...])


def _ffn_down(x, w_down, h, l):
    T, K = x.shape
    D = w_down.shape[2]
    tm, tn = _tile(T, 512), _tile(D, 512)
    return pl.pallas_call(
        functools.partial(_ffn_down_body, row_chunk=_tile(K, 256)),
        grid=(D // tn, T // tm),
        in_specs=[
            pl.BlockSpec((tm, K), lambda j, i: (i, 0)),
            pl.BlockSpec((None, K, tn), lambda j, i: (l, 0, j)),
            pl.BlockSpec((tm, tn), lambda j, i: (i, j)),
        ],
        out_specs=pl.BlockSpec((tm, tn), lambda j, i: (i, j)),
        out_shape=jax.ShapeDtypeStruct((T, D), F32),
        scratch_shapes=[pltpu.VMEM((K, tn), BF16)],
        compiler_params=_params("arbitrary", "arbitrary"),
        name="ffn_down",
    )(x, w_down, h)


def kernel(x, norm_mix, w_in, w_alpha2, b_alpha2, gla_norm, w_o_gla, conv_w, conv_b, conv_norm_g,
           conv_norm_b, w_pw2, b_pw2, w_out, norm_ffn, w_gate, w_up, w_down, norm_final):
    B, S, D = x.shape
    L = w_in.shape[0]
    T = B * S
    rank, key_w = w_alpha2.shape[1], w_alpha2.shape[2]
    val_w = gla_norm.shape[1]
    ch = conv_w.shape[2]
    alr_off = 2 * key_w + 2 * val_w
    glu_off = alr_off
    gate_off = glu_off + 2 * ch

    def rows(p):
        return p.reshape(p.shape[0], 1, p.shape[1])

    wa = jnp.pad(w_alpha2, ((0, 0), (0, LANES - rank), (0, 0))).astype(BF16)
    norm_mix3, norm_ffn3 = rows(norm_mix), rows(norm_ffn)
    ba3, gn3 = rows(b_alpha2), rows(gla_norm)
    cb3, lg3, lb3, bpw3 = rows(conv_b), rows(conv_norm_g), rows(conv_norm_b), rows(b_pw2)

    h = x.reshape(T, D)
    for l in range(L):
        xn, alr = _mix_norm(h, norm_mix3, w_in, l, alr_off=alr_off)
        proj = _in_proj(xn, w_in, l, alr_off=alr_off, rank=rank)
        proj3 = proj.reshape(B, S, proj.shape[1])
        xa = _gla(alr.reshape(B, S, LANES), proj3, wa, ba3, gn3, l, key_w=key_w, val_w=val_w)
        u = _conv_branch(proj3, conv_w, cb3, lg3, lb3, l, glu_off=glu_off, ch=ch)
        merged = _merge(xa.reshape(T, val_w), u.reshape(T, ch), proj, w_o_gla, w_pw2, bpw3, l,
                        gate_off=gate_off)
        h, xn_ffn = _out_proj(merged, w_out, h, norm_ffn3, l)
        act = _ffn_up(xn_ffn, w_gate, w_up, l)
        h = _ffn_down(act, w_down, h, l)
    out = _norm(h, norm_final.reshape(1, 1, D), 0, F32, "final_norm")
    return out.reshape(B, S, D)
```

```python
import functools

import jax
import jax.numpy as jnp
from jax import lax
from jax.experimental import pallas as pl
from jax.experimental.pallas import tpu as pltpu

F32 = jnp.float32
BF16 = jnp.bfloat16

RMS_EPS = 1e-6
LN_EPS = 1e-5
GLA_HEADS = 4
GLA_CHUNK = 64
GLA_GATE_TAU = 16.0

LANES = 128
SUBLANES = 8
VMEM_LIMIT_BYTES = 56 * 1024 * 1024


def _params(*sem):
    return pltpu.CompilerParams(dimension_semantics=sem, vmem_limit_bytes=VMEM_LIMIT_BYTES)


def _tile(n, pref):
    t = min(n, pref)
    while n % t:
        t //= 2
    return t


def _dot(a, b):
    return jnp.dot(a, b, preferred_element_type=F32)


def _dot_nt(a, b):
    return lax.dot_general(a, b, (((1,), (1,)), ((), ())), preferred_element_type=F32)


def _dot_tn(a, b):
    return lax.dot_general(a, b, (((0,), (0,)), ((), ())), preferred_element_type=F32)


def _sigmoid(x):
    return 0.5 * jnp.tanh(0.5 * x) + 0.5


def _rmsnorm_rows(h_ref, g_ref, out_ref, row_chunk):
    g = g_ref[...]

    def body(i, carry):
        rows = pl.ds(pl.multiple_of(i * row_chunk, row_chunk), row_chunk)
        x = h_ref[rows, :]
        ms = jnp.mean(x * x, axis=-1, keepdims=True)
        out_ref[rows, :] = (x * lax.rsqrt(ms + RMS_EPS) * g).astype(out_ref.dtype)
        return carry

    lax.fori_loop(0, h_ref.shape[0] // row_chunk, body, 0, unroll=2)


def _cast_rows(w_ref, out_ref, row_chunk):
    def body(i, carry):
        rows = pl.ds(pl.multiple_of(i * row_chunk, row_chunk), row_chunk)
        out_ref[rows, :] = w_ref[rows, :].astype(out_ref.dtype)
        return carry

    lax.fori_loop(0, w_ref.shape[0] // row_chunk, body, 0)


def _vec_spec(n, l, nidx):
    return pl.BlockSpec((None, 1, n), lambda *idx: (l, 0, 0))


def _norm_body(h_ref, g_ref, o_ref, *, row_chunk):
    _rmsnorm_rows(h_ref, g_ref, o_ref, row_chunk)


def _norm(h, gains, l, out_dtype, name):
    T, D = h.shape
    tm = _tile(T, 512)
    return pl.pallas_call(
        functools.partial(_norm_body, row_chunk=_tile(tm, 64)),
        grid=(T // tm,),
        in_specs=[pl.BlockSpec((tm, D), lambda i: (i, 0)), _vec_spec(D, l, 1)],
        out_specs=pl.BlockSpec((tm, D), lambda i: (i, 0)),
        out_shape=jax.ShapeDtypeStruct((T, D), out_dtype),
        compiler_params=_params("parallel"),
        name=name,
    )(h, gains)


def _mix_norm_body(h_ref, g_ref, walr_ref, xn_ref, alr_ref, walr_scr, *, row_chunk):
    @pl.when(pl.program_id(0) == 0)
    def _():
        walr_scr[...] = walr_ref[...].astype(BF16)

    _rmsnorm_rows(h_ref, g_ref, xn_ref, row_chunk)
    alr_ref[...] = _dot_nt(xn_ref[...], walr_scr[...])


def _mix_norm(h, gains, w_in_t, l, *, alr_off):
    T, D = h.shape
    tm = _tile(T, 512)
    assert alr_off % LANES == 0
    return pl.pallas_call(
        functools.partial(_mix_norm_body, row_chunk=_tile(tm, 64)),
        grid=(T // tm,),
        in_specs=[
            pl.BlockSpec((tm, D), lambda i: (i, 0)),
            _vec_spec(D, l, 1),
            pl.BlockSpec((None, LANES, D), lambda i: (l, alr_off // LANES, 0)),
        ],
        out_specs=[pl.BlockSpec((tm, D), lambda i: (i, 0)), pl.BlockSpec((tm, LANES), lambda i: (i, 0))],
        out_shape=[jax.ShapeDtypeStruct((T, D), BF16), jax.ShapeDtypeStruct((T, LANES), F32)],
        scratch_shapes=[pltpu.VMEM((LANES, D), BF16)],
        compiler_params=_params("arbitrary"),
        name="mix_norm",
    )(h, gains, w_in_t)


def _in_proj_body(x_ref, wt_ref, o_ref, wscr_ref, *, row_chunk):
    @pl.when(pl.program_id(1) == 0)
    def _():
        _cast_rows(wt_ref.at[0], wscr_ref, row_chunk)

    o_ref[...] = _dot_nt(x_ref[...], wscr_ref[...]).astype(o_ref.dtype)


def _in_proj(xn, w_in_t, l, *, alr_off, rank):
    T, D = xn.shape
    NW = w_in_t.shape[1] - rank
    tm, tn = _tile(T, 2048), _tile(NW, 1024)
    assert alr_off % tn == 0 and rank % SUBLANES == 0
    skip_from = alr_off // tn

    def w_rows(j, i):
        return (l, (j * (tn // SUBLANES) + jnp.where(j >= skip_from, rank // SUBLANES, 0)) * SUBLANES, 0)

    return pl.pallas_call(
        functools.partial(_in_proj_body, row_chunk=_tile(tn, 64)),
        grid=(NW // tn, T // tm),
        in_specs=[
            pl.BlockSpec((tm, D), lambda j, i: (i, 0)),
            pl.BlockSpec((pl.Element(1), pl.Element(tn), pl.Element(D)), w_rows),
        ],
        out_specs=pl.BlockSpec((tm, tn), lambda j, i: (i, j)),
        out_shape=jax.ShapeDtypeStruct((T, NW), BF16),
        scratch_shapes=[pltpu.VMEM((tn, D), BF16)],
        compiler_params=_params("arbitrary", "arbitrary"),
        name="in_proj",
    )(xn, w_in_t)


def _log_sigmoid(z):
    return jnp.minimum(z, 0.0) - jnp.log1p(jnp.exp(-jnp.abs(z)))


def _cumsum_rows(tril, x):
    hi = x.astype(BF16)
    r1 = x - hi.astype(F32)
    mid = r1.astype(BF16)
    lo = (r1 - mid.astype(F32)).astype(BF16)
    return _dot(tril, hi) + _dot(tril, mid) + _dot(tril, lo)


def _gla_body(alr_ref, q_ref, k_ref, v_ref, r_ref, wa_ref, ba_ref, gn_ref, o_ref,
              state_ref, att_ref, *, n_chunks, n_heads, scale):
    C = GLA_CHUNK
    NC = n_chunks
    _, DK, DV = state_ref.shape
    TS = NC * C
    heads = range(n_heads)
    kcs = [slice(hh * DK, (hh + 1) * DK) for hh in heads]
    vcs = [slice(hh * DV, (hh + 1) * DV) for hh in heads]
    chunk = [slice(c * C, (c + 1) * C) for c in range(NC)]

    @pl.when(pl.program_id(2) == 0)
    def _():
        state_ref[...] = jnp.zeros_like(state_ref)

    row = lax.broadcasted_iota(jnp.int32, (TS, TS), 0)
    col = lax.broadcasted_iota(jnp.int32, (TS, TS), 1)
    chunk_shift = C.bit_length() - 1
    causal = (row >= col) & ((row >> chunk_shift) == (col >> chunk_shift))
    tril = causal.astype(BF16)

    z = _dot(alr_ref[...].astype(BF16), wa_ref[...]) + ba_ref[...]
    log_a = _log_sigmoid(z) / GLA_GATE_TAU
    b = _cumsum_rows(tril, log_a)
    bl = [b[(c + 1) * C - 1:(c + 1) * C, :] for c in range(NC)]
    E = [jnp.zeros_like(bl[0])]
    for c in range(NC):
        E.append(E[c] + bl[c])
    bl_rows = jnp.concatenate([jnp.broadcast_to(x, (C, x.shape[1])) for x in bl], axis=0)
    qf = q_ref[...].astype(F32) * scale * jnp.exp(b)
    kf = k_ref[...].astype(F32)
    kdf = kf * jnp.exp(bl_rows - b)
    qe = qf.astype(BF16)
    ke = (kf * jnp.exp(-b)).astype(BF16)
    kd = kdf.astype(BF16)
    cross_lhs = []
    for j in range(NC - 1):
        parts = [qe[chunk[j + 1], :]]
        parts += [(qf[chunk[c], :] * jnp.exp(E[c] - E[j + 1])).astype(BF16) for c in range(j + 2, NC)]
        cross_lhs.append(jnp.concatenate(parts, axis=0))
    qs = jnp.concatenate(
        [qe[chunk[0], :]] + [(qf[chunk[c], :] * jnp.exp(E[c])).astype(BF16) for c in range(1, NC)], axis=0)
    kds = jnp.concatenate(
        [(kdf[chunk[c], :] * jnp.exp(E[NC] - E[c + 1])).astype(BF16) for c in range(NC - 1)]
        + [kd[chunk[NC - 1], :]], axis=0)
    g_end = jnp.exp(E[NC])

    for hh in heads:
        att_ref[hh] = jnp.where(causal, _dot_nt(qe[:, kcs[hh]], ke[:, kcs[hh]]), 0.0)
    for j in range(NC - 1):
        for hh in heads:
            att_ref[hh, (j + 1) * C:TS, chunk[j]] = _dot_nt(cross_lhs[j][:, kcs[hh]], kd[chunk[j], kcs[hh]])
    o = []
    for hh in heads:
        lhs_o = jnp.concatenate([att_ref[hh].astype(BF16), qs[:, kcs[hh]]], axis=1)
        rhs_o = jnp.concatenate([v_ref[:, vcs[hh]], state_ref[hh].astype(BF16)], axis=0)
        o.append(_dot(lhs_o, rhs_o))
    lane_blk = min(LANES, DV)
    for hh in heads:
        upd = _dot_tn(kds[:, kcs[hh]], v_ref[:, vcs[hh]])
        g_col = jnp.broadcast_to(g_end[:, kcs[hh]], (lane_blk, DK)).T
        for jb in range(DV // lane_blk):
            cols = slice(jb * lane_blk, (jb + 1) * lane_blk)
            state_ref[hh, :, cols] = state_ref[hh, :, cols] * g_col + upd[:, cols]
    for hh in heads:
        ms = jnp.mean(o[hh] * o[hh], axis=-1, keepdims=True)
        on = o[hh] * lax.rsqrt(ms + RMS_EPS) * gn_ref[:, vcs[hh]]
        r = r_ref[:, vcs[hh]].astype(F32)
        o_ref[:, vcs[hh]] = (on * (r * _sigmoid(r))).astype(o_ref.dtype)


def _gla(alr3, proj3, wa, ba, gn, l, *, key_w, val_w):
    B, S, RP = alr3.shape
    H = GLA_HEADS
    DK, DV = key_w // H, val_w // H
    ts = _tile(S, 256)
    hps = 2
    KB, VB = hps * DK, hps * DV
    assert ts % GLA_CHUNK == 0 and H % hps == 0 and (2 * key_w) % VB == 0 and RP == wa.shape[1]
    k_blk0 = key_w // KB
    v_blk0 = (2 * key_w) // VB
    r_blk0 = (2 * key_w + val_w) // VB
    return pl.pallas_call(
        functools.partial(_gla_body, n_chunks=ts // GLA_CHUNK, n_heads=hps, scale=DK ** -0.5),
        grid=(B, H // hps, S // ts),
        in_specs=[
            pl.BlockSpec((None, ts, RP), lambda b, h, s: (b, s, 0)),
            pl.BlockSpec((None, ts, KB), lambda b, h, s: (b, s, h)),
            pl.BlockSpec((None, ts, KB), lambda b, h, s: (b, s, k_blk0 + h)),
            pl.BlockSpec((None, ts, VB), lambda b, h, s: (b, s, v_blk0 + h)),
            pl.BlockSpec((None, ts, VB), lambda b, h, s: (b, s, r_blk0 + h)),
            pl.BlockSpec((None, RP, KB), lambda b, h, s: (l, 0, h)),
            pl.BlockSpec((None, 1, KB), lambda b, h, s: (l, 0, h)),
            pl.BlockSpec((None, 1, VB), lambda b, h, s: (l, 0, h)),
        ],
        out_specs=pl.BlockSpec((None, ts, VB), lambda b, h, s: (b, s, h)),
        out_shape=jax.ShapeDtypeStruct((B, S, val_w), BF16),
        scratch_shapes=[pltpu.VMEM((hps, DK, DV), F32), pltpu.VMEM((hps, ts, ts), F32)],
        compiler_params=_params("parallel", "parallel", "arbitrary"),
        name="gla",
    )(alr3, proj3, proj3, proj3, proj3, wa, ba, gn)


def _conv_body(val_ref, gate_ref, w_ref, cb_ref, lg_ref, lb_ref, o_ref, uext_ref, sh_ref, y_ref,
               *, halo, row_blk, ln_blk):
    ts, CH = val_ref.shape
    W = w_ref.shape[0]
    n_ext = halo + ts
    step = pl.program_id(1)

    @pl.when(step == 0)
    def _():
        uext_ref[0:halo, :] = jnp.zeros((halo, CH), F32)

    @pl.when(step > 0)
    def _():
        uext_ref[0:halo, :] = uext_ref[ts:ts + halo, :]

    uext_ref[halo:halo + ts, :] = val_ref[...].astype(F32) * _sigmoid(gate_ref[...].astype(F32))

    lane_blk = min(LANES, CH)
    first = halo - (W - 1)

    def conv_lanes(lb, carry):
        lanes = pl.ds(pl.multiple_of(lb * lane_blk, lane_blk), lane_blk)
        x = uext_ref[:, lanes]
        for s in range(1, SUBLANES):
            sh_ref[s] = pltpu.roll(x, n_ext - s, axis=0)
        taps = [w_ref[j:j + 1, lanes] for j in range(W)]
        bias = cb_ref[:, lanes]
        for rb in range(ts // row_blk):
            acc = jnp.broadcast_to(bias, (row_blk, lane_blk))
            for j in range(W):
                s = (first + j) % SUBLANES
                base = rb * row_blk + (first + j) - s
                if s == 0:
                    src = uext_ref[pl.ds(base, row_blk), lanes]
                else:
                    src = sh_ref[s, pl.ds(base, row_blk), :]
                acc = acc + taps[j] * src
            y_ref[rb * row_blk:(rb + 1) * row_blk, lanes] = acc
        return carry

    lax.fori_loop(0, CH // lane_blk, conv_lanes, 0)

    lg = lg_ref[...]
    lb_ = lb_ref[...]

    def ln_rows(i, carry):
        rows = pl.ds(pl.multiple_of(i * ln_blk, ln_blk), ln_blk)
        x = y_ref[rows, :]
        mu = jnp.mean(x, axis=-1, keepdims=True)
        xc = x - mu
        var = jnp.mean(xc * xc, axis=-1, keepdims=True)
        y = xc * lax.rsqrt(var + LN_EPS) * lg + lb_
        o_ref[rows, :] = (y * _sigmoid(y)).astype(o_ref.dtype)
        return carry

    lax.fori_loop(0, ts // ln_blk, ln_rows, 0, unroll=True)


def _conv_branch(proj3, conv_w, conv_b, ln_g, ln_b, l, *, glu_off, ch):
    B, S, _ = proj3.shape
    W = conv_w.shape[1]
    halo = -(-(W - 1) // SUBLANES) * SUBLANES
    ts = _tile(S, 256)
    assert ts >= halo and glu_off % ch == 0
    blk0 = glu_off // ch
    return pl.pallas_call(
        functools.partial(_conv_body, halo=halo, row_blk=_tile(ts, 64), ln_blk=_tile(ts, 16)),
        grid=(B, S // ts),
        in_specs=[
            pl.BlockSpec((None, ts, ch), lambda b, s: (b, s, blk0)),
            pl.BlockSpec((None, ts, ch), lambda b, s: (b, s, blk0 + 1)),
            pl.BlockSpec((None, W, ch), lambda b, s: (l, 0, 0)),
            _vec_spec(ch, l, 2), _vec_spec(ch, l, 2), _vec_spec(ch, l, 2),
        ],
        out_specs=pl.BlockSpec((None, ts, ch), lambda b, s: (b, s, 0)),
        out_shape=jax.ShapeDtypeStruct((B, S, ch), BF16),
        scratch_shapes=[pltpu.VMEM((halo + ts, ch), F32),
                        pltpu.VMEM((SUBLANES, halo + ts, min(LANES, ch)), F32),
                        pltpu.VMEM((ts, ch), F32)],
        compiler_params=_params("parallel", "arbitrary"),
        name="conv_branch",
    )(proj3, proj3, conv_w, conv_b, ln_g, ln_b)


def _merge_body(xa_ref, u_ref, ga_ref, gb_ref, woa_ref, wpw_ref, bpw_ref, o_ref, woa_scr, wpw_scr,
                *, row_chunk):
    @pl.when(pl.program_id(1) == 0)
    def _():
        _cast_rows(woa_ref, woa_scr, row_chunk)
        _cast_rows(wpw_ref, wpw_scr, row_chunk)

    a = _dot(xa_ref[...], woa_scr[...])
    bb = _dot(u_ref[...], wpw_scr[...]) + bpw_ref[...]
    ga = _sigmoid(ga_ref[...].astype(F32))
    gb = _sigmoid(gb_ref[...].astype(F32))
    o_ref[...] = (ga * a + gb * bb).astype(o_ref.dtype)


def _merge(xa, u, proj, w_o_gla, w_pw2, b_pw2, l, *, gate_off):
    T, VW = xa.shape
    CH = u.shape[1]
    D = w_o_gla.shape[2]
    tm, tn = _tile(T, 1024), _tile(D, 512)
    assert gate_off % tn == 0
    ga0 = gate_off // tn
    gb0 = (gate_off + D) // tn
    return pl.pallas_call(
        functools.partial(_merge_body, row_chunk=_tile(min(VW, CH), 256)),
        grid=(D // tn, T // tm),
        in_specs=[
            pl.BlockSpec((tm, VW), lambda j, i: (i, 0)),
            pl.BlockSpec((tm, CH), lambda j, i: (i, 0)),
            pl.BlockSpec((tm, tn), lambda j, i: (i, ga0 + j)),
            pl.BlockSpec((tm, tn), lambda j, i: (i, gb0 + j)),
            pl.BlockSpec((None, VW, tn), lambda j, i: (l, 0, j)),
            pl.BlockSpec((None, CH, tn), lambda j, i: (l, 0, j)),
            pl.BlockSpec((None, 1, tn), lambda j, i: (l, 0, j)),
        ],
        out_specs=pl.BlockSpec((tm, tn), lambda j, i: (i, j)),
        out_shape=jax.ShapeDtypeStruct((T, D), BF16),
        scratch_shapes=[pltpu.VMEM((VW, tn), BF16), pltpu.VMEM((CH, tn), BF16)],
        compiler_params=_params("arbitrary", "arbitrary"),
        name="merge",
    )(xa, u, proj, proj, w_o_gla, w_pw2, b_pw2)


def _out_proj_body(x_ref, w_ref, h_ref, g_ref, ho_ref, xn_ref, wscr_ref, *, row_chunk):
    @pl.when(pl.program_id(0) == 0)
    def _():
        _cast_rows(w_ref, wscr_ref, row_chunk)

    ho_ref[...] = h_ref[...] + _dot(x_ref[...], wscr_ref[...])
    _rmsnorm_rows(ho_ref, g_ref, xn_ref, _tile(ho_ref.shape[0], 64))


def _out_proj(x, w_out, h, gains, l):
    T, K = x.shape
    D = w_out.shape[2]
    tm = _tile(T, 512)
    return pl.pallas_call(
        functools.partial(_out_proj_body, row_chunk=_tile(K, 256)),
        grid=(T // tm,),
        in_specs=[
            pl.BlockSpec((tm, K), lambda i: (i, 0)),
            pl.BlockSpec((None, K, D), lambda i: (l, 0, 0), pipeline_mode=pl.Buffered(1)),
            pl.BlockSpec((tm, D), lambda i: (i, 0)),
            _vec_spec(D, l, 1),
        ],
        out_specs=[pl.BlockSpec((tm, D), lambda i: (i, 0)), pl.BlockSpec((tm, D), lambda i: (i, 0))],
        out_shape=[jax.ShapeDtypeStruct((T, D), F32), jax.ShapeDtypeStruct((T, D), BF16)],
        scratch_shapes=[pltpu.VMEM((K, D), BF16)],
        compiler_params=_params("arbitrary"),
        name="out_proj",
    )(x, w_out, h, gains)


def _ffn_up_body(x_ref, wg_ref, wu_ref, o_ref, wg_scr, wu_scr, *, row_chunk):
    @pl.when(pl.program_id(1) == 0)
    def _():
        _cast_rows(wg_ref, wg_scr, row_chunk)
        _cast_rows(wu_ref, wu_scr, row_chunk)

    xn = x_ref[...]
    gate = _dot(xn, wg_scr[...])
    up = _dot(xn, wu_scr[...])
    o_ref[...] = (gate * _sigmoid(gate) * up).astype(o_ref.dtype)


def _ffn_up(xn, w_gate, w_up, l):
    T, D = xn.shape
    F = w_gate.shape[2]
    tm, tn = _tile(T, 2048), _tile(F, 512)
    return pl.pallas_call(
        functools.partial(_ffn_up_body, row_chunk=_tile(D, 256)),
        grid=(F // tn, T // tm),
        in_specs=[
            pl.BlockSpec((tm, D), lambda j, i: (i, 0)),
            pl.BlockSpec((None, D, tn), lambda j, i: (l, 0, j)),
            pl.BlockSpec((None, D, tn), lambda j, i: (l, 0, j)),
        ],
        out_specs=pl.BlockSpec((tm, tn), lambda j, i: (i, j)),
        out_shape=jax.ShapeDtypeStruct((T, F), BF16),
        scratch_shapes=[pltpu.VMEM((D, tn), BF16), pltpu.VMEM((D, tn), BF16)],
        compiler_params=_params("arbitrary", "arbitrary"),
        name="ffn_up",
    )(xn, w_gate, w_up)


def _ffn_down_body(x_ref, w_ref, h_ref, o_ref, wscr_ref, *, row_chunk):
    @pl.when(pl.program_id(1) == 0)
    def _():
        _cast_rows(w_ref, wscr_ref, row_chunk)

    o_ref[...] = h_ref[...] + _dot(x_ref[...], wscr_ref[...])


def _ffn_down(x, w_down, h, l):
    T, K = x.shape
    D = w_down.shape[2]
    tm, tn = _tile(T, 1024), _tile(D, 512)
    return pl.pallas_call(
        functools.partial(_ffn_down_body, row_chunk=_tile(K, 256)),
        grid=(D // tn, T // tm),
        in_specs=[
            pl.BlockSpec((tm, K), lambda j, i: (i, 0)),
            pl.BlockSpec((None, K, tn), lambda j, i: (l, 0, j), pipeline_mode=pl.Buffered(1)),
            pl.BlockSpec((tm, tn), lambda j, i: (i, j)),
        ],
        out_specs=pl.BlockSpec((tm, tn), lambda j, i: (i, j)),
        out_shape=jax.ShapeDtypeStruct((T, D), F32),
        scratch_shapes=[pltpu.VMEM((K, tn), BF16)],
        compiler_params=_params("arbitrary", "arbitrary"),
        name="ffn_down",
    )(x, w_down, h)


def kernel(x, norm_mix, w_in, w_alpha2, b_alpha2, gla_norm, w_o_gla, conv_w, conv_b, conv_norm_g,
           conv_norm_b, w_pw2, b_pw2, w_out, norm_ffn, w_gate, w_up, w_down, norm_final):
    B, S, D = x.shape
    L = w_in.shape[0]
    T = B * S
    rank, key_w = w_alpha2.shape[1], w_alpha2.shape[2]
    val_w = gla_norm.shape[1]
    ch = conv_w.shape[2]
    alr_off = 2 * key_w + 2 * val_w
    glu_off = alr_off
    gate_off = glu_off + 2 * ch

    def rows(p):
        return p.reshape(p.shape[0], 1, p.shape[1])

    wa = jnp.pad(w_alpha2, ((0, 0), (0, LANES - rank), (0, 0))).astype(BF16)
    norm_mix3, norm_ffn3 = rows(norm_mix), rows(norm_ffn)
    ba3, gn3 = rows(b_alpha2), rows(gla_norm)
    cb3, lg3, lb3, bpw3 = rows(conv_b), rows(conv_norm_g), rows(conv_norm_b), rows(b_pw2)

    w_in_t = jnp.swapaxes(w_in, 1, 2)
    h = x.reshape(T, D)
    for l in range(L):
        xn, alr = _mix_norm(h, norm_mix3, w_in_t, l, alr_off=alr_off)
        proj = _in_proj(xn, w_in_t, l, alr_off=alr_off, rank=rank)
        proj3 = proj.reshape(B, S, proj.shape[1])
        xa = _gla(alr.reshape(B, S, LANES), proj3, wa, ba3, gn3, l, key_w=key_w, val_w=val_w)
        u = _conv_branch(proj3, conv_w, cb3, lg3, lb3, l, glu_off=glu_off, ch=ch)
        merged = _merge(xa.reshape(T, val_w), u.reshape(T, ch), proj, w_o_gla, w_pw2, bpw3, l,
                        gate_off=gate_off)
        h, xn_ffn = _out_proj(merged, w_out, h, norm_ffn3, l)
        act = _ffn_up(xn_ffn, w_gate, w_up, l)
        h = _ffn_down(act, w_down, h, l)
    out = _norm(h, norm_final.reshape(1, 1, D), 0, F32, "final_norm")
    return out.reshape(B, S, D)
```

```python
import functools

import jax
import jax.numpy as jnp
from jax import lax
from jax.experimental import pallas as pl
from jax.experimental.pallas import tpu as pltpu

F32 = jnp.float32
BF16 = jnp.bfloat16

RMS_EPS = 1e-6
LN_EPS = 1e-5
GLA_HEADS = 4
GLA_CHUNK = 64
GLA_GATE_TAU = 16.0

LANES = 128
SUBLANES = 8
VMEM_LIMIT_BYTES = 56 * 1024 * 1024


def _params(*sem):
    return pltpu.CompilerParams(dimension_semantics=sem, vmem_limit_bytes=VMEM_LIMIT_BYTES)


def _tile(n, pref):
    t = min(n, pref)
    while n % t:
        t //= 2
    return t


def _dot(a, b):
    return jnp.dot(a, b, preferred_element_type=F32)


def _dot_nt(a, b):
    return lax.dot_general(a, b, (((1,), (1,)), ((), ())), preferred_element_type=F32)


def _dot_tn(a, b):
    return lax.dot_general(a, b, (((0,), (0,)), ((), ())), preferred_element_type=F32)


def _sigmoid(x):
    return 0.5 * jnp.tanh(0.5 * x) + 0.5


def _rmsnorm_rows(h_ref, g_ref, out_ref, row_chunk):
    g = g_ref[...]

    def body(i, carry):
        rows = pl.ds(pl.multiple_of(i * row_chunk, row_chunk), row_chunk)
        x = h_ref[rows, :]
        ms = jnp.mean(x * x, axis=-1, keepdims=True)
        out_ref[rows, :] = (x * lax.rsqrt(ms + RMS_EPS) * g).astype(out_ref.dtype)
        return carry

    lax.fori_loop(0, h_ref.shape[0] // row_chunk, body, 0, unroll=2)


def _cast_rows(w_ref, out_ref, row_chunk):
    def body(i, carry):
        rows = pl.ds(pl.multiple_of(i * row_chunk, row_chunk), row_chunk)
        out_ref[rows, :] = w_ref[rows, :].astype(out_ref.dtype)
        return carry

    lax.fori_loop(0, w_ref.shape[0] // row_chunk, body, 0)


def _vec_spec(n, l):
    return pl.BlockSpec((None, 1, n), lambda *idx: (l, 0, 0))


def _norm_body(h_ref, g_ref, o_ref, *, row_chunk):
    _rmsnorm_rows(h_ref, g_ref, o_ref, row_chunk)


def _norm(h, gains, l, out_dtype, name):
    T, D = h.shape
    tm = _tile(T, 512)
    return pl.pallas_call(
        functools.partial(_norm_body, row_chunk=_tile(tm, 64)),
        grid=(T // tm,),
        in_specs=[pl.BlockSpec((tm, D), lambda i: (i, 0)), _vec_spec(D, l)],
        out_specs=pl.BlockSpec((tm, D), lambda i: (i, 0)),
        out_shape=jax.ShapeDtypeStruct((T, D), out_dtype),
        compiler_params=_params("parallel"),
        name=name,
    )(h, gains)


def _mix_norm_body(h_ref, g_ref, walr_ref, xn_ref, alr_ref, walr_scr, *, row_chunk):
    @pl.when(pl.program_id(0) == 0)
    def _():
        walr_scr[...] = walr_ref[...].astype(BF16)

    _rmsnorm_rows(h_ref, g_ref, xn_ref, row_chunk)
    alr_ref[...] = _dot_nt(xn_ref[...], walr_scr[...])


def _mix_norm(h, gains, w_in_t, l, *, alr_off):
    T, D = h.shape
    tm = _tile(T, 512)
    assert alr_off % LANES == 0
    return pl.pallas_call(
        functools.partial(_mix_norm_body, row_chunk=_tile(tm, 64)),
        grid=(T // tm,),
        in_specs=[
            pl.BlockSpec((tm, D), lambda i: (i, 0)),
            _vec_spec(D, l),
            pl.BlockSpec((None, LANES, D), lambda i: (l, alr_off // LANES, 0)),
        ],
        out_specs=[pl.BlockSpec((tm, D), lambda i: (i, 0)), pl.BlockSpec((tm, LANES), lambda i: (i, 0))],
        out_shape=[jax.ShapeDtypeStruct((T, D), BF16), jax.ShapeDtypeStruct((T, LANES), F32)],
        scratch_shapes=[pltpu.VMEM((LANES, D), BF16)],
        compiler_params=_params("arbitrary"),
        name="mix_norm",
    )(h, gains, w_in_t)


def _in_proj_body(x_ref, wt_ref, o_ref, wscr_ref, *, row_chunk):
    @pl.when(pl.program_id(1) == 0)
    def _():
        _cast_rows(wt_ref.at[0], wscr_ref, row_chunk)

    o_ref[...] = _dot_nt(x_ref[...], wscr_ref[...]).astype(o_ref.dtype)


def _in_proj(xn, w_in_t, l, *, alr_off, rank):
    T, D = xn.shape
    NW = w_in_t.shape[1] - rank
    tm, tn = _tile(T, 2048), _tile(NW, 1024)
    assert alr_off % tn == 0 and rank % SUBLANES == 0
    skip_from = alr_off // tn

    def w_rows(j, i):
        return (l, (j * (tn // SUBLANES) + jnp.where(j >= skip_from, rank // SUBLANES, 0)) * SUBLANES, 0)

    return pl.pallas_call(
        functools.partial(_in_proj_body, row_chunk=_tile(tn, 64)),
        grid=(NW // tn, T // tm),
        in_specs=[
            pl.BlockSpec((tm, D), lambda j, i: (i, 0)),
            pl.BlockSpec((pl.Element(1), pl.Element(tn), pl.Element(D)), w_rows),
        ],
        out_specs=pl.BlockSpec((tm, tn), lambda j, i: (i, j)),
        out_shape=jax.ShapeDtypeStruct((T, NW), BF16),
        scratch_shapes=[pltpu.VMEM((tn, D), BF16)],
        compiler_params=_params("arbitrary", "arbitrary"),
        name="in_proj",
    )(xn, w_in_t)


def _log_sigmoid(z):
    return jnp.minimum(z, 0.0) - jnp.log1p(jnp.exp(-jnp.abs(z)))


def _cumsum_rows(tril, x):
    hi = x.astype(BF16)
    r1 = x - hi.astype(F32)
    mid = r1.astype(BF16)
    lo = (r1 - mid.astype(F32)).astype(BF16)
    return _dot(tril, hi) + _dot(tril, mid) + _dot(tril, lo)


def _gla_body(alr_ref, q_ref, k_ref, v_ref, r_ref, wa_ref, ba_ref, gn_ref, o_ref,
              state_ref, att_ref, *, n_chunks, n_heads, scale):
    C = GLA_CHUNK
    NC = n_chunks
    _, DK, DV = state_ref.shape
    TS = NC * C
    heads = range(n_heads)
    kcs = [slice(hh * DK, (hh + 1) * DK) for hh in heads]
    vcs = [slice(hh * DV, (hh + 1) * DV) for hh in heads]
    chunk = [slice(c * C, (c + 1) * C) for c in range(NC)]

    @pl.when(pl.program_id(2) == 0)
    def _():
        state_ref[...] = jnp.zeros_like(state_ref)

    row = lax.broadcasted_iota(jnp.int32, (TS, TS), 0)
    col = lax.broadcasted_iota(jnp.int32, (TS, TS), 1)
    chunk_shift = C.bit_length() - 1
    causal = (row >= col) & ((row >> chunk_shift) == (col >> chunk_shift))
    tril = causal.astype(BF16)

    z = _dot(alr_ref[...].astype(BF16), wa_ref[...]) + ba_ref[...]
    log_a = _log_sigmoid(z) / GLA_GATE_TAU
    b = _cumsum_rows(tril, log_a)
    bl = [b[(c + 1) * C - 1:(c + 1) * C, :] for c in range(NC)]
    E = [jnp.zeros_like(bl[0])]
    for c in range(NC):
        E.append(E[c] + bl[c])
    bl_rows = jnp.concatenate([jnp.broadcast_to(x, (C, x.shape[1])) for x in bl], axis=0)
    qf = q_ref[...].astype(F32) * scale * jnp.exp(b)
    kf = k_ref[...].astype(F32)
    kdf = kf * jnp.exp(bl_rows - b)
    qe = qf.astype(BF16)
    ke = (kf * jnp.exp(-b)).astype(BF16)
    kd = kdf.astype(BF16)
    cross_lhs = []
    for j in range(NC - 1):
        parts = [qe[chunk[j + 1], :]]
        parts += [(qf[chunk[c], :] * jnp.exp(E[c] - E[j + 1])).astype(BF16) for c in range(j + 2, NC)]
        cross_lhs.append(jnp.concatenate(parts, axis=0))
    qs = jnp.concatenate(
        [qe[chunk[0], :]] + [(qf[chunk[c], :] * jnp.exp(E[c])).astype(BF16) for c in range(1, NC)], axis=0)
    kds = jnp.concatenate(
        [(kdf[chunk[c], :] * jnp.exp(E[NC] - E[c + 1])).astype(BF16) for c in range(NC - 1)]
        + [kd[chunk[NC - 1], :]], axis=0)
    g_end = jnp.exp(E[NC])

    for hh in heads:
        att_ref[hh] = jnp.where(causal, _dot_nt(qe[:, kcs[hh]], ke[:, kcs[hh]]), 0.0)
    for j in range(NC - 1):
        for hh in heads:
            att_ref[hh, (j + 1) * C:TS, chunk[j]] = _dot_nt(cross_lhs[j][:, kcs[hh]], kd[chunk[j], kcs[hh]])
    o = []
    for hh in heads:
        lhs_o = jnp.concatenate([att_ref[hh].astype(BF16), qs[:, kcs[hh]]], axis=1)
        rhs_o = jnp.concatenate([v_ref[:, vcs[hh]], state_ref[hh].astype(BF16)], axis=0)
        o.append(_dot(lhs_o, rhs_o))
    lane_blk = min(LANES, DV)
    for hh in heads:
        upd = _dot_tn(kds[:, kcs[hh]], v_ref[:, vcs[hh]])
        g_col = jnp.broadcast_to(g_end[:, kcs[hh]], (lane_blk, DK)).T
        for jb in range(DV // lane_blk):
            cols = slice(jb * lane_blk, (jb + 1) * lane_blk)
            state_ref[hh, :, cols] = state_ref[hh, :, cols] * g_col + upd[:, cols]
    for hh in heads:
        ms = jnp.mean(o[hh] * o[hh], axis=-1, keepdims=True)
        on = o[hh] * lax.rsqrt(ms + RMS_EPS) * gn_ref[:, vcs[hh]]
        r = r_ref[:, vcs[hh]].astype(F32)
        o_ref[:, vcs[hh]] = (on * (r * _sigmoid(r))).astype(o_ref.dtype)


def _gla(alr3, proj3, wa, ba, gn, l, *, key_w, val_w):
    B, S, RP = alr3.shape
    H = GLA_HEADS
    DK, DV = key_w // H, val_w // H
    ts = _tile(S, 256)
    hps = H
    KB, VB = hps * DK, hps * DV
    assert ts % GLA_CHUNK == 0 and (2 * key_w) % VB == 0 and RP == wa.shape[1]
    k_blk0 = key_w // KB
    v_blk0 = (2 * key_w) // VB
    r_blk0 = (2 * key_w + val_w) // VB
    return pl.pallas_call(
        functools.partial(_gla_body, n_chunks=ts // GLA_CHUNK, n_heads=hps, scale=DK ** -0.5),
        grid=(B, H // hps, S // ts),
        in_specs=[
            pl.BlockSpec((None, ts, RP), lambda b, h, s: (b, s, 0)),
            pl.BlockSpec((None, ts, KB), lambda b, h, s: (b, s, h)),
            pl.BlockSpec((None, ts, KB), lambda b, h, s: (b, s, k_blk0 + h)),
            pl.BlockSpec((None, ts, VB), lambda b, h, s: (b, s, v_blk0 + h)),
            pl.BlockSpec((None, ts, VB), lambda b, h, s: (b, s, r_blk0 + h)),
            pl.BlockSpec((None, RP, KB), lambda b, h, s: (l, 0, h)),
            pl.BlockSpec((None, 1, KB), lambda b, h, s: (l, 0, h)),
            pl.BlockSpec((None, 1, VB), lambda b, h, s: (l, 0, h)),
        ],
        out_specs=pl.BlockSpec((None, ts, VB), lambda b, h, s: (b, s, h)),
        out_shape=jax.ShapeDtypeStruct((B, S, val_w), BF16),
        scratch_shapes=[pltpu.VMEM((hps, DK, DV), F32), pltpu.VMEM((hps, ts, ts), F32)],
        compiler_params=_params("parallel", "parallel", "arbitrary"),
        name="gla",
    )(alr3, proj3, proj3, proj3, proj3, wa, ba, gn)


_CONV_PITCH = 4


def _conv_body(val_ref, gate_ref, w_ref, cb_ref, lg_ref, lb_ref, o_ref, uext_ref, y_ref,
               *, halo, row_blk, ln_blk):
    ts, CH = val_ref.shape
    W = w_ref.shape[0]
    n_lb, _, lane_blk = uext_ref.shape
    step = pl.program_id(1)

    @pl.when(step == 0)
    def _():
        uext_ref[:, 0:halo, :] = jnp.zeros((n_lb, halo, lane_blk), F32)

    @pl.when(step > 0)
    def _():
        uext_ref[:, 0:halo, :] = uext_ref[:, ts:ts + halo, :]

    u = val_ref[...].astype(F32) * _sigmoid(gate_ref[...].astype(F32))
    for lb in range(n_lb):
        uext_ref[lb, halo:halo + ts, :] = u[:, lb * lane_blk:(lb + 1) * lane_blk]

    first = halo - (W - 1)
    span = _CONV_PITCH * SUBLANES

    def conv_lanes(lb, carry):
        lanes = pl.ds(pl.multiple_of(lb * lane_blk, lane_blk), lane_blk)
        taps = [w_ref[j:j + 1, lanes] for j in range(W)]
        bias = jnp.broadcast_to(cb_ref[:, lanes], (SUBLANES, lane_blk))
        for rb in range(ts // row_blk):
            starts = [rb * row_blk + g * span + p for g in range(row_blk // span) for p in range(_CONV_PITCH)]
            accs = [bias for _ in starts]
            for j in range(W):
                for a, st in enumerate(starts):
                    src = uext_ref[lb, pl.ds(first + j + st, SUBLANES, stride=_CONV_PITCH), :]
                    accs[a] = accs[a] + taps[j] * src
            for a, st in enumerate(starts):
                y_ref[lb, pl.ds(st, SUBLANES, stride=_CONV_PITCH), :] = accs[a]
        return carry

    lax.fori_loop(0, n_lb, conv_lanes, 0)

    lg = lg_ref[...]
    lb_ = lb_ref[...]

    def ln_rows(i, carry):
        rows = pl.ds(pl.multiple_of(i * ln_blk, ln_blk), ln_blk)
        x = jnp.concatenate([y_ref[lb, rows, :] for lb in range(n_lb)], axis=1)
        mu = jnp.mean(x, axis=-1, keepdims=True)
        xc = x - mu
        var = jnp.mean(xc * xc, axis=-1, keepdims=True)
        y = xc * lax.rsqrt(var + LN_EPS) * lg + lb_
        o_ref[rows, :] = (y * _sigmoid(y)).astype(o_ref.dtype)
        return carry

    lax.fori_loop(0, ts // ln_blk, ln_rows, 0, unroll=True)


def _conv_branch(proj3, conv_w, conv_b, ln_g, ln_b, l, *, glu_off, ch):
    B, S, _ = proj3.shape
    W = conv_w.shape[1]
    halo = -(-(W - 1) // SUBLANES) * SUBLANES
    ts = _tile(S, 256)
    lane_blk = min(LANES, ch)
    assert ts >= halo and glu_off % ch == 0 and ts % (_CONV_PITCH * SUBLANES) == 0
    blk0 = glu_off // ch
    return pl.pallas_call(
        functools.partial(_conv_body, halo=halo, row_blk=_CONV_PITCH * SUBLANES, ln_blk=_tile(ts, 16)),
        grid=(B, S // ts),
        in_specs=[
            pl.BlockSpec((None, ts, ch), lambda b, s: (b, s, blk0)),
            pl.BlockSpec((None, ts, ch), lambda b, s: (b, s, blk0 + 1)),
            pl.BlockSpec((None, W, ch), lambda b, s: (l, 0, 0)),
            _vec_spec(ch, l), _vec_spec(ch, l), _vec_spec(ch, l),
        ],
        out_specs=pl.BlockSpec((None, ts, ch), lambda b, s: (b, s, 0)),
        out_shape=jax.ShapeDtypeStruct((B, S, ch), BF16),
        scratch_shapes=[pltpu.VMEM((ch // lane_blk, halo + ts, lane_blk), F32),
                        pltpu.VMEM((ch // lane_blk, ts, lane_blk), F32)],
        compiler_params=_params("parallel", "arbitrary"),
        name="conv_branch",
    )(proj3, proj3, conv_w, conv_b, ln_g, ln_b)


def _merge_body(xa_ref, u_ref, ga_ref, gb_ref, woa_ref, wpw_ref, bpw_ref, o_ref, woa_scr, wpw_scr,
                *, row_chunk):
    @pl.when(pl.program_id(1) == 0)
    def _():
        _cast_rows(woa_ref, woa_scr, row_chunk)
        _cast_rows(wpw_ref, wpw_scr, row_chunk)

    a = _dot(xa_ref[...], woa_scr[...])
    bb = _dot(u_ref[...], wpw_scr[...]) + bpw_ref[...]
    ga = _sigmoid(ga_ref[...].astype(F32))
    gb = _sigmoid(gb_ref[...].astype(F32))
    o_ref[...] = (ga * a + gb * bb).astype(o_ref.dtype)


def _merge(xa, u, proj, w_o_gla, w_pw2, b_pw2, l, *, gate_off):
    T, VW = xa.shape
    CH = u.shape[1]
    D = w_o_gla.shape[2]
    tm, tn = _tile(T, 1024), _tile(D, 512)
    assert gate_off % tn == 0
    ga0 = gate_off // tn
    gb0 = (gate_off + D) // tn
    return pl.pallas_call(
        functools.partial(_merge_body, row_chunk=_tile(min(VW, CH), 256)),
        grid=(D // tn, T // tm),
        in_specs=[
            pl.BlockSpec((tm, VW), lambda j, i: (i, 0)),
            pl.BlockSpec((tm, CH), lambda j, i: (i, 0)),
            pl.BlockSpec((tm, tn), lambda j, i: (i, ga0 + j)),
            pl.BlockSpec((tm, tn), lambda j, i: (i, gb0 + j)),
            pl.BlockSpec((None, VW, tn), lambda j, i: (l, 0, j)),
            pl.BlockSpec((None, CH, tn), lambda j, i: (l, 0, j)),
            pl.BlockSpec((None, 1, tn), lambda j, i: (l, 0, j)),
        ],
        out_specs=pl.BlockSpec((tm, tn), lambda j, i: (i, j)),
        out_shape=jax.ShapeDtypeStruct((T, D), BF16),
        scratch_shapes=[pltpu.VMEM((VW, tn), BF16), pltpu.VMEM((CH, tn), BF16)],
        compiler_params=_params("arbitrary", "arbitrary"),
        name="merge",
    )(xa, u, proj, proj, w_o_gla, w_pw2, b_pw2)


def _out_proj_body(x_ref, w_ref, h_ref, g_ref, ho_ref, xn_ref, wscr_ref, *, row_chunk):
    @pl.when(pl.program_id(0) == 0)
    def _():
        _cast_rows(w_ref, wscr_ref, row_chunk)

    ho_ref[...] = h_ref[...] + _dot(x_ref[...], wscr_ref[...])
    _rmsnorm_rows(ho_ref, g_ref, xn_ref, _tile(ho_ref.shape[0], 64))


def _out_proj(x, w_out, h, gains, l):
    T, K = x.shape
    D = w_out.shape[2]
    tm = _tile(T, 512)
    return pl.pallas_call(
        functools.partial(_out_proj_body, row_chunk=_tile(K, 256)),
        grid=(T // tm,),
        in_specs=[
            pl.BlockSpec((tm, K), lambda i: (i, 0)),
            pl.BlockSpec((None, K, D), lambda i: (l, 0, 0), pipeline_mode=pl.Buffered(1)),
            pl.BlockSpec((tm, D), lambda i: (i, 0)),
            _vec_spec(D, l),
        ],
        out_specs=[pl.BlockSpec((tm, D), lambda i: (i, 0)), pl.BlockSpec((tm, D), lambda i: (i, 0))],
        out_shape=[jax.ShapeDtypeStruct((T, D), F32), jax.ShapeDtypeStruct((T, D), BF16)],
        scratch_shapes=[pltpu.VMEM((K, D), BF16)],
        compiler_params=_params("arbitrary"),
        name="out_proj",
    )(x, w_out, h, gains)


def _ffn_up_body(x_ref, wg_ref, wu_ref, o_ref, wg_scr, wu_scr, *, row_chunk):
    @pl.when(pl.program_id(1) == 0)
    def _():
        _cast_rows(wg_ref, wg_scr, row_chunk)
        _cast_rows(wu_ref, wu_scr, row_chunk)

    xn = x_ref[...]
    gate = _dot(xn, wg_scr[...])
    up = _dot(xn, wu_scr[...])
    o_ref[...] = (gate * _sigmoid(gate) * up).astype(o_ref.dtype)


def _ffn_up(xn, w_gate, w_up, l):
    T, D = xn.shape
    F = w_gate.shape[2]
    tm, tn = _tile(T, 1024), _tile(F, 512)
    return pl.pallas_call(
        functools.partial(_ffn_up_body, row_chunk=_tile(D, 256)),
        grid=(F // tn, T // tm),
        in_specs=[
            pl.BlockSpec((tm, D), lambda j, i: (i, 0)),
            pl.BlockSpec((None, D, tn), lambda j, i: (l, 0, j)),
            pl.BlockSpec((None, D, tn), lambda j, i: (l, 0, j)),
        ],
        out_specs=pl.BlockSpec((tm, tn), lambda j, i: (i, j)),
        out_shape=jax.ShapeDtypeStruct((T, F), BF16),
        scratch_shapes=[pltpu.VMEM((D, tn), BF16), pltpu.VMEM((D, tn), BF16)],
        compiler_params=_params("arbitrary", "arbitrary"),
        name="ffn_up",
    )(xn, w_gate, w_up)


def _ffn_down_body(x_ref, w_ref, h_ref, o_ref, wscr_ref, *, row_chunk):
    @pl.when(pl.program_id(1) == 0)
    def _():
        _cast_rows(w_ref, wscr_ref, row_chunk)

    o_ref[...] = h_ref[...] + _dot(x_ref[...], wscr_ref[...])


def _ffn_down(x, w_down, h, l):
    T, K = x.shape
    D = w_down.shape[2]
    tm, tn = _tile(T, 1024), _tile(D, 512)
    return pl.pallas_call(
        functools.partial(_ffn_down_body, row_chunk=_tile(K, 256)),
        grid=(D // tn, T // tm),
        in_specs=[
            pl.BlockSpec((tm, K), lambda j, i: (i, 0)),
            pl.BlockSpec((None, K, tn), lambda j, i: (l, 0, j), pipeline_mode=pl.Buffered(1)),
            pl.BlockSpec((tm, tn), lambda j, i: (i, j)),
        ],
        out_specs=pl.BlockSpec((tm, tn), lambda j, i: (i, j)),
        out_shape=jax.ShapeDtypeStruct((T, D), F32),
        scratch_shapes=[pltpu.VMEM((K, tn), BF16)],
        compiler_params=_params("arbitrary", "arbitrary"),
        name="ffn_down",
    )(x, w_down, h)


def kernel(x, norm_mix, w_in, w_alpha2, b_alpha2, gla_norm, w_o_gla, conv_w, conv_b, conv_norm_g,
           conv_norm_b, w_pw2, b_pw2, w_out, norm_ffn, w_gate, w_up, w_down, norm_final):
    B, S, D = x.shape
    L = w_in.shape[0]
    T = B * S
    rank, key_w = w_alpha2.shape[1], w_alpha2.shape[2]
    val_w = gla_norm.shape[1]
    ch = conv_w.shape[2]
    alr_off = 2 * key_w + 2 * val_w
    glu_off = alr_off
    gate_off = glu_off + 2 * ch

    def rows(p):
        return p.reshape(p.shape[0], 1, p.shape[1])

    wa = jnp.pad(w_alpha2, ((0, 0), (0, LANES - rank), (0, 0))).astype(BF16)
    norm_mix3, norm_ffn3 = rows(norm_mix), rows(norm_ffn)
    ba3, gn3 = rows(b_alpha2), rows(gla_norm)
    cb3, lg3, lb3, bpw3 = rows(conv_b), rows(conv_norm_g), rows(conv_norm_b), rows(b_pw2)

    w_in_t = jnp.swapaxes(w_in, 1, 2)
    h = x.reshape(T, D)
    for l in range(L):
        xn, alr = _mix_norm(h, norm_mix3, w_in_t, l, alr_off=alr_off)
        proj = _in_proj(xn, w_in_t, l, alr_off=alr_off, rank=rank)
        proj3 = proj.reshape(B, S, proj.shape[1])
        xa = _gla(alr.reshape(B, S, LANES), proj3, wa, ba3, gn3, l, key_w=key_w, val_w=val_w)
        u = _conv_branch(proj3, conv_w, cb3, lg3, lb3, l, glu_off=glu_off, ch=ch)
        merged = _merge(xa.reshape(T, val_w), u.reshape(T, ch), proj, w_o_gla, w_pw2, bpw3, l,
                        gate_off=gate_off)
        h, xn_ffn = _out_proj(merged, w_out, h, norm_ffn3, l)
        act = _ffn_up(xn_ffn, w_gate, w_up, l)
        h = _ffn_down(act, w_down, h, l)
    out = _norm(h, norm_final.reshape(1, 1, D), 0, F32, "final_norm")
    return out.reshape(B, S, D)
```

```python
import functools

import jax
import jax.numpy as jnp
from jax import lax
from jax.experimental import pallas as pl
from jax.experimental.pallas import tpu as pltpu

F32 = jnp.float32
BF16 = jnp.bfloat16

RMS_EPS = 1e-6
LN_EPS = 1e-5
GLA_HEADS = 4
GLA_CHUNK = 64
GLA_GATE_TAU = 16.0

LANES = 128
SUBLANES = 8
VMEM_LIMIT_BYTES = 56 * 1024 * 1024


def _params(*sem, vmem_limit_bytes=VMEM_LIMIT_BYTES):
    return pltpu.CompilerParams(dimension_semantics=sem, vmem_limit_bytes=vmem_limit_bytes)


def _tile(n, pref):
    t = min(n, pref)
    while n % t:
        t //= 2
    return t


def _dot(a, b):
    return jnp.dot(a, b, preferred_element_type=F32)


def _dot_nt(a, b):
    return lax.dot_general(a, b, (((1,), (1,)), ((), ())), preferred_element_type=F32)


def _dot_tn(a, b):
    return lax.dot_general(a, b, (((0,), (0,)), ((), ())), preferred_element_type=F32)


def _sigmoid(x):
    return 0.5 * jnp.tanh(0.5 * x) + 0.5


def _silu(x):
    h = 0.5 * x
    return h + h * jnp.tanh(h)


def _rmsnorm_rows(h_ref, g_ref, out_ref, row_chunk):
    g = g_ref[...]

    def body(i, carry):
        rows = pl.ds(pl.multiple_of(i * row_chunk, row_chunk), row_chunk)
        x = h_ref[rows, :]
        ms = jnp.mean(x * x, axis=-1, keepdims=True)
        out_ref[rows, :] = (x * lax.rsqrt(ms + RMS_EPS) * g).astype(out_ref.dtype)
        return carry

    lax.fori_loop(0, h_ref.shape[0] // row_chunk, body, 0, unroll=2)


def _cast_rows(w_ref, out_ref, row_chunk):
    def body(i, carry):
        rows = pl.ds(pl.multiple_of(i * row_chunk, row_chunk), row_chunk)
        out_ref[rows, :] = w_ref[rows, :].astype(out_ref.dtype)
        return carry

    lax.fori_loop(0, w_ref.shape[0] // row_chunk, body, 0)


def _vec_spec(n, l):
    return pl.BlockSpec((None, 1, n), lambda *idx: (l, 0, 0))


def _norm_body(h_ref, g_ref, o_ref, *, row_chunk):
    _rmsnorm_rows(h_ref, g_ref, o_ref, row_chunk)


def _norm(h, gains, l, out_dtype, name):
    T, D = h.shape
    tm = _tile(T, 512)
    return pl.pallas_call(
        functools.partial(_norm_body, row_chunk=_tile(tm, 64)),
        grid=(T // tm,),
        in_specs=[pl.BlockSpec((tm, D), lambda i: (i, 0)), _vec_spec(D, l)],
        out_specs=pl.BlockSpec((tm, D), lambda i: (i, 0)),
        out_shape=jax.ShapeDtypeStruct((T, D), out_dtype),
        compiler_params=_params("parallel"),
        name=name,
    )(h, gains)


def _mix_norm_body(h_ref, g_ref, walr_ref, xn_ref, alr_ref, walr_scr, *, row_chunk):
    @pl.when(pl.program_id(0) == 0)
    def _():
        walr_scr[...] = walr_ref[...].astype(BF16)

    _rmsnorm_rows(h_ref, g_ref, xn_ref, row_chunk)
    alr_ref[...] = _dot_nt(xn_ref[...], walr_scr[...])


def _mix_norm(h, gains, w_in_t, l, *, alr_off):
    T, D = h.shape
    tm = _tile(T, 512)
    assert alr_off % LANES == 0
    return pl.pallas_call(
        functools.partial(_mix_norm_body, row_chunk=_tile(tm, 64)),
        grid=(T // tm,),
        in_specs=[
            pl.BlockSpec((tm, D), lambda i: (i, 0)),
            _vec_spec(D, l),
            pl.BlockSpec((None, LANES, D), lambda i: (l, alr_off // LANES, 0)),
        ],
        out_specs=[pl.BlockSpec((tm, D), lambda i: (i, 0)), pl.BlockSpec((tm, LANES), lambda i: (i, 0))],
        out_shape=[jax.ShapeDtypeStruct((T, D), BF16), jax.ShapeDtypeStruct((T, LANES), F32)],
        scratch_shapes=[pltpu.VMEM((LANES, D), BF16)],
        compiler_params=_params("arbitrary"),
        name="mix_norm",
    )(h, gains, w_in_t)


def _in_proj_body(x_ref, wt_ref, o_ref, wscr_ref, *, row_chunk):
    @pl.when(pl.program_id(1) == 0)
    def _():
        _cast_rows(wt_ref.at[0], wscr_ref, row_chunk)

    o_ref[...] = _dot_nt(x_ref[...], wscr_ref[...]).astype(o_ref.dtype)


def _in_proj(xn, w_in_t, l, *, alr_off, rank):
    T, D = xn.shape
    NW = w_in_t.shape[1] - rank
    tm, tn = _tile(T, 2048), _tile(NW, 1024)
    assert alr_off % tn == 0 and rank % SUBLANES == 0
    skip_from = alr_off // tn

    def w_rows(j, i):
        return (l, (j * (tn // SUBLANES) + jnp.where(j >= skip_from, rank // SUBLANES, 0)) * SUBLANES, 0)

    return pl.pallas_call(
        functools.partial(_in_proj_body, row_chunk=_tile(tn, 64)),
        grid=(NW // tn, T // tm),
        in_specs=[
            pl.BlockSpec((tm, D), lambda j, i: (i, 0)),
            pl.BlockSpec((pl.Element(1), pl.Element(tn), pl.Element(D)), w_rows),
        ],
        out_specs=pl.BlockSpec((tm, tn), lambda j, i: (i, j)),
        out_shape=jax.ShapeDtypeStruct((T, NW), BF16),
        scratch_shapes=[pltpu.VMEM((tn, D), BF16)],
        compiler_params=_params("arbitrary", "arbitrary"),
        name="in_proj",
    )(xn, w_in_t)


def _log_sigmoid(z):
    return jnp.minimum(z, 0.0) - jnp.log1p(jnp.exp(-jnp.abs(z)))


def _cumsum_rows(tril, x):
    hi = x.astype(BF16)
    r1 = x - hi.astype(F32)
    mid = r1.astype(BF16)
    lo = (r1 - mid.astype(F32)).astype(BF16)
    return _dot(tril, hi) + _dot(tril, mid) + _dot(tril, lo)


def _gla_body(alr_ref, q_ref, k_ref, v_ref, r_ref, wa_ref, ba_ref, gn_ref, o_ref,
              state_ref, att_ref, *, n_chunks, n_heads, scale):
    C = GLA_CHUNK
    NC = n_chunks
    _, DK, DV = state_ref.shape
    TS = NC * C
    heads = range(n_heads)
    kcs = [slice(hh * DK, (hh + 1) * DK) for hh in heads]
    vcs = [slice(hh * DV, (hh + 1) * DV) for hh in heads]
    chunk = [slice(c * C, (c + 1) * C) for c in range(NC)]

    @pl.when(pl.program_id(2) == 0)
    def _():
        state_ref[...] = jnp.zeros_like(state_ref)

    row = lax.broadcasted_iota(jnp.int32, (TS, TS), 0)
    col = lax.broadcasted_iota(jnp.int32, (TS, TS), 1)
    chunk_shift = C.bit_length() - 1
    causal = (row >= col) & ((row >> chunk_shift) == (col >> chunk_shift))
    tril = causal.astype(BF16)

    z = _dot(alr_ref[...].astype(BF16), wa_ref[...]) + ba_ref[...]
    log_a = _log_sigmoid(z) / GLA_GATE_TAU
    b = _cumsum_rows(tril, log_a)
    bl = [b[(c + 1) * C - 1:(c + 1) * C, :] for c in range(NC)]
    E = [jnp.zeros_like(bl[0])]
    for c in range(NC):
        E.append(E[c] + bl[c])
    bl_rows = jnp.concatenate([jnp.broadcast_to(x, (C, x.shape[1])) for x in bl], axis=0)
    qf = q_ref[...].astype(F32) * scale * jnp.exp(b)
    kf = k_ref[...].astype(F32)
    kdf = kf * jnp.exp(bl_rows - b)
    qe = qf.astype(BF16)
    ke = (kf * jnp.exp(-b)).astype(BF16)
    kd = kdf.astype(BF16)
    cross_lhs = []
    for j in range(NC - 1):
        parts = [qe[chunk[j + 1], :]]
        parts += [(qf[chunk[c], :] * jnp.exp(E[c] - E[j + 1])).astype(BF16) for c in range(j + 2, NC)]
        cross_lhs.append(jnp.concatenate(parts, axis=0))
    qs = jnp.concatenate(
        [qe[chunk[0], :]] + [(qf[chunk[c], :] * jnp.exp(E[c])).astype(BF16) for c in range(1, NC)], axis=0)
    kds = jnp.concatenate(
        [(kdf[chunk[c], :] * jnp.exp(E[NC] - E[c + 1])).astype(BF16) for c in range(NC - 1)]
        + [kd[chunk[NC - 1], :]], axis=0)
    g_end = jnp.exp(E[NC])

    for hh in heads:
        att_ref[hh] = jnp.where(causal, _dot_nt(qe[:, kcs[hh]], ke[:, kcs[hh]]), 0.0)
    for j in range(NC - 1):
        for hh in heads:
            att_ref[hh, (j + 1) * C:TS, chunk[j]] = _dot_nt(cross_lhs[j][:, kcs[hh]], kd[chunk[j], kcs[hh]])
    o = []
    for hh in heads:
        lhs_o = jnp.concatenate([att_ref[hh].astype(BF16), qs[:, kcs[hh]]], axis=1)
        rhs_o = jnp.concatenate([v_ref[:, vcs[hh]], state_ref[hh].astype(BF16)], axis=0)
        o.append(_dot(lhs_o, rhs_o))
    lane_blk = min(LANES, DV)
    for hh in heads:
        upd = _dot_tn(kds[:, kcs[hh]], v_ref[:, vcs[hh]])
        g_col = jnp.broadcast_to(g_end[:, kcs[hh]], (lane_blk, DK)).T
        for jb in range(DV // lane_blk):
            cols = slice(jb * lane_blk, (jb + 1) * lane_blk)
            state_ref[hh, :, cols] = state_ref[hh, :, cols] * g_col + upd[:, cols]
    for hh in heads:
        ms = jnp.mean(o[hh] * o[hh], axis=-1, keepdims=True)
        on = o[hh] * lax.rsqrt(ms + RMS_EPS) * gn_ref[:, vcs[hh]]
        r = r_ref[:, vcs[hh]].astype(F32)
        o_ref[:, vcs[hh]] = (on * _silu(r)).astype(o_ref.dtype)


def _gla(alr3, proj3, wa, ba, gn, l, *, key_w, val_w):
    B, S, RP = alr3.shape
    H = GLA_HEADS
    DK, DV = key_w // H, val_w // H
    ts = _tile(S, 256)
    hps = H
    KB, VB = hps * DK, hps * DV
    assert ts % GLA_CHUNK == 0 and (2 * key_w) % VB == 0 and RP == wa.shape[1]
    k_blk0 = key_w // KB
    v_blk0 = (2 * key_w) // VB
    r_blk0 = (2 * key_w + val_w) // VB
    return pl.pallas_call(
        functools.partial(_gla_body, n_chunks=ts // GLA_CHUNK, n_heads=hps, scale=DK ** -0.5),
        grid=(B, H // hps, S // ts),
        in_specs=[
            pl.BlockSpec((None, ts, RP), lambda b, h, s: (b, s, 0)),
            pl.BlockSpec((None, ts, KB), lambda b, h, s: (b, s, h)),
            pl.BlockSpec((None, ts, KB), lambda b, h, s: (b, s, k_blk0 + h)),
            pl.BlockSpec((None, ts, VB), lambda b, h, s: (b, s, v_blk0 + h)),
            pl.BlockSpec((None, ts, VB), lambda b, h, s: (b, s, r_blk0 + h)),
            pl.BlockSpec((None, RP, KB), lambda b, h, s: (l, 0, h)),
            pl.BlockSpec((None, 1, KB), lambda b, h, s: (l, 0, h)),
            pl.BlockSpec((None, 1, VB), lambda b, h, s: (l, 0, h)),
        ],
        out_specs=pl.BlockSpec((None, ts, VB), lambda b, h, s: (b, s, h)),
        out_shape=jax.ShapeDtypeStruct((B, S, val_w), BF16),
        scratch_shapes=[pltpu.VMEM((hps, DK, DV), F32), pltpu.VMEM((hps, ts, ts), F32)],
        compiler_params=_params("parallel", "parallel", "arbitrary"),
        name="gla",
    )(alr3, proj3, proj3, proj3, proj3, wa, ba, gn)


_CONV_PITCH = 4


def _conv_body(val_ref, gate_ref, w_ref, cb_ref, lg_ref, lb_ref, o_ref, uext_ref, y_ref,
               *, halo, row_blk, ln_blk):
    ts, CH = val_ref.shape
    W = w_ref.shape[0]
    n_lb, _, lane_blk = uext_ref.shape
    step = pl.program_id(1)

    @pl.when(step == 0)
    def _():
        uext_ref[:, 0:halo, :] = jnp.zeros((n_lb, halo, lane_blk), F32)

    @pl.when(step > 0)
    def _():
        uext_ref[:, 0:halo, :] = uext_ref[:, ts:ts + halo, :]

    u = val_ref[...].astype(F32) * _sigmoid(gate_ref[...].astype(F32))
    for lb in range(n_lb):
        uext_ref[lb, halo:halo + ts, :] = u[:, lb * lane_blk:(lb + 1) * lane_blk]

    first = halo - (W - 1)
    span = _CONV_PITCH * SUBLANES

    def conv_lanes(lb, carry):
        lanes = pl.ds(pl.multiple_of(lb * lane_blk, lane_blk), lane_blk)
        taps = [w_ref[j:j + 1, lanes] for j in range(W)]
        bias = jnp.broadcast_to(cb_ref[:, lanes], (SUBLANES, lane_blk))
        for rb in range(ts // row_blk):
            starts = [rb * row_blk + g * span + p for g in range(row_blk // span) for p in range(_CONV_PITCH)]
            accs = [bias for _ in starts]
            for j in range(W):
                for a, st in enumerate(starts):
                    src = uext_ref[lb, pl.ds(first + j + st, SUBLANES, stride=_CONV_PITCH), :]
                    accs[a] = accs[a] + taps[j] * src
            for a, st in enumerate(starts):
                y_ref[lb, pl.ds(st, SUBLANES, stride=_CONV_PITCH), :] = accs[a]
        return carry

    lax.fori_loop(0, n_lb, conv_lanes, 0)

    lg = lg_ref[...]
    lb_ = lb_ref[...]

    def ln_rows(i, carry):
        rows = pl.ds(pl.multiple_of(i * ln_blk, ln_blk), ln_blk)
        x = jnp.concatenate([y_ref[lb, rows, :] for lb in range(n_lb)], axis=1)
        mu = jnp.mean(x, axis=-1, keepdims=True)
        xc = x - mu
        var = jnp.mean(xc * xc, axis=-1, keepdims=True)
        y = xc * lax.rsqrt(var + LN_EPS) * lg + lb_
        o_ref[rows, :] = _silu(y).astype(o_ref.dtype)
        return carry

    lax.fori_loop(0, ts // ln_blk, ln_rows, 0, unroll=True)


def _conv_branch(proj3, conv_w, conv_b, ln_g, ln_b, l, *, glu_off, ch):
    B, S, _ = proj3.shape
    W = conv_w.shape[1]
    halo = -(-(W - 1) // SUBLANES) * SUBLANES
    ts = _tile(S, 256)
    lane_blk = min(LANES, ch)
    assert ts >= halo and glu_off % ch == 0 and ts % (_CONV_PITCH * SUBLANES) == 0
    blk0 = glu_off // ch
    return pl.pallas_call(
        functools.partial(_conv_body, halo=halo, row_blk=_CONV_PITCH * SUBLANES, ln_blk=_tile(ts, 16)),
        grid=(B, S // ts),
        in_specs=[
            pl.BlockSpec((None, ts, ch), lambda b, s: (b, s, blk0)),
            pl.BlockSpec((None, ts, ch), lambda b, s: (b, s, blk0 + 1)),
            pl.BlockSpec((None, W, ch), lambda b, s: (l, 0, 0)),
            _vec_spec(ch, l), _vec_spec(ch, l), _vec_spec(ch, l),
        ],
        out_specs=pl.BlockSpec((None, ts, ch), lambda b, s: (b, s, 0)),
        out_shape=jax.ShapeDtypeStruct((B, S, ch), BF16),
        scratch_shapes=[pltpu.VMEM((ch // lane_blk, halo + ts, lane_blk), F32),
                        pltpu.VMEM((ch // lane_blk, ts, lane_blk), F32)],
        compiler_params=_params("parallel", "arbitrary"),
        name="conv_branch",
    )(proj3, proj3, conv_w, conv_b, ln_g, ln_b)


def _merge_body(xa_ref, u_ref, ga_ref, gb_ref, woa_ref, wpw_ref, bpw_ref, o_ref, woa_scr, wpw_scr,
                *, row_chunk):
    @pl.when(pl.program_id(1) == 0)
    def _():
        _cast_rows(woa_ref, woa_scr, row_chunk)
        _cast_rows(wpw_ref, wpw_scr, row_chunk)

    a = _dot(xa_ref[...], woa_scr[...])
    bb = _dot(u_ref[...], wpw_scr[...]) + bpw_ref[...]
    ga = _sigmoid(ga_ref[...].astype(F32))
    gb = _sigmoid(gb_ref[...].astype(F32))
    o_ref[...] = (ga * a + gb * bb).astype(o_ref.dtype)


def _merge(xa, u, proj, w_o_gla, w_pw2, b_pw2, l, *, gate_off):
    T, VW = xa.shape
    CH = u.shape[1]
    D = w_o_gla.shape[2]
    tm, tn = _tile(T, 1024), _tile(D, 512)
    assert gate_off % tn == 0
    ga0 = gate_off // tn
    gb0 = (gate_off + D) // tn
    return pl.pallas_call(
        functools.partial(_merge_body, row_chunk=_tile(min(VW, CH), 256)),
        grid=(D // tn, T // tm),
        in_specs=[
            pl.BlockSpec((tm, VW), lambda j, i: (i, 0)),
            pl.BlockSpec((tm, CH), lambda j, i: (i, 0)),
            pl.BlockSpec((tm, tn), lambda j, i: (i, ga0 + j)),
            pl.BlockSpec((tm, tn), lambda j, i: (i, gb0 + j)),
            pl.BlockSpec((None, VW, tn), lambda j, i: (l, 0, j)),
            pl.BlockSpec((None, CH, tn), lambda j, i: (l, 0, j)),
            pl.BlockSpec((None, 1, tn), lambda j, i: (l, 0, j)),
        ],
        out_specs=pl.BlockSpec((tm, tn), lambda j, i: (i, j)),
        out_shape=jax.ShapeDtypeStruct((T, D), BF16),
        scratch_shapes=[pltpu.VMEM((VW, tn), BF16), pltpu.VMEM((CH, tn), BF16)],
        compiler_params=_params("arbitrary", "arbitrary"),
        name="merge",
    )(xa, u, proj, proj, w_o_gla, w_pw2, b_pw2)


def _out_proj_body(x_ref, w_ref, h_ref, g_ref, ho_ref, xn_ref, wscr_ref, *, row_chunk):
    @pl.when(pl.program_id(0) == 0)
    def _():
        _cast_rows(w_ref, wscr_ref, row_chunk)

    ho_ref[...] = h_ref[...] + _dot(x_ref[...], wscr_ref[...])
    _rmsnorm_rows(ho_ref, g_ref, xn_ref, _tile(ho_ref.shape[0], 64))


def _out_proj(x, w_out, h, gains, l):
    T, K = x.shape
    D = w_out.shape[2]
    tm = _tile(T, 512)
    return pl.pallas_call(
        functools.partial(_out_proj_body, row_chunk=_tile(K, 256)),
        grid=(T // tm,),
        in_specs=[
            pl.BlockSpec((tm, K), lambda i: (i, 0)),
            pl.BlockSpec((None, K, D), lambda i: (l, 0, 0), pipeline_mode=pl.Buffered(1)),
            pl.BlockSpec((tm, D), lambda i: (i, 0)),
            _vec_spec(D, l),
        ],
        out_specs=[pl.BlockSpec((tm, D), lambda i: (i, 0)), pl.BlockSpec((tm, D), lambda i: (i, 0))],
        out_shape=[jax.ShapeDtypeStruct((T, D), F32), jax.ShapeDtypeStruct((T, D), BF16)],
        scratch_shapes=[pltpu.VMEM((K, D), BF16)],
        compiler_params=_params("arbitrary"),
        name="out_proj",
    )(x, w_out, h, gains)


def _ffn_up_body(x_ref, wg_ref, wu_ref, o_ref, wg_scr, wu_scr, *, row_chunk):
    @pl.when(pl.program_id(1) == 0)
    def _():
        _cast_rows(wg_ref, wg_scr, row_chunk)
        _cast_rows(wu_ref, wu_scr, row_chunk)

    xn = x_ref[...]
    gate = _dot(xn, wg_scr[...])
    up = _dot(xn, wu_scr[...])
    o_ref[...] = (_silu(gate) * up).astype(o_ref.dtype)


def _ffn_up(xn, w_gate, w_up, l):
    T, D = xn.shape
    F = w_gate.shape[2]
    tm, tn = _tile(T, 1024), _tile(F, 512)
    return pl.pallas_call(
        functools.partial(_ffn_up_body, row_chunk=_tile(D, 256)),
        grid=(F // tn, T // tm),
        in_specs=[
            pl.BlockSpec((tm, D), lambda j, i: (i, 0)),
            pl.BlockSpec((None, D, tn), lambda j, i: (l, 0, j)),
            pl.BlockSpec((None, D, tn), lambda j, i: (l, 0, j)),
        ],
        out_specs=pl.BlockSpec((tm, tn), lambda j, i: (i, j)),
        out_shape=jax.ShapeDtypeStruct((T, F), BF16),
        scratch_shapes=[pltpu.VMEM((D, tn), BF16), pltpu.VMEM((D, tn), BF16)],
        compiler_params=_params("arbitrary", "arbitrary"),
        name="ffn_up",
    )(xn, w_gate, w_up)


def _ffn_down_body(x_ref, w_ref, h_ref, o_ref, wscr_ref, *, row_chunk):
    @pl.when(pl.program_id(1) == 0)
    def _():
        _cast_rows(w_ref, wscr_ref, row_chunk)

    o_ref[...] = h_ref[...] + _dot(x_ref[...], wscr_ref[...])


def _ffn_down(x, w_down, h, l):
    T, K = x.shape
    D = w_down.shape[2]
    tm, tn = _tile(T, 512), _tile(D, 1024)
    return pl.pallas_call(
        functools.partial(_ffn_down_body, row_chunk=_tile(K, 256)),
        grid=(D // tn, T // tm),
        in_specs=[
            pl.BlockSpec((tm, K), lambda j, i: (i, 0)),
            pl.BlockSpec((None, K, tn), lambda j, i: (l, 0, j), pipeline_mode=pl.Buffered(1)),
            pl.BlockSpec((tm, tn), lambda j, i: (i, j)),
        ],
        out_specs=pl.BlockSpec((tm, tn), lambda j, i: (i, j)),
        out_shape=jax.ShapeDtypeStruct((T, D), F32),
        scratch_shapes=[pltpu.VMEM((K, tn), BF16)],
        compiler_params=_params("arbitrary", "arbitrary"),
        name="ffn_down",
    )(x, w_down, h)


def kernel(x, norm_mix, w_in, w_alpha2, b_alpha2, gla_norm, w_o_gla, conv_w, conv_b, conv_norm_g,
           conv_norm_b, w_pw2, b_pw2, w_out, norm_ffn, w_gate, w_up, w_down, norm_final):
    B, S, D = x.shape
    L = w_in.shape[0]
    T = B * S
    rank, key_w = w_alpha2.shape[1], w_alpha2.shape[2]
    val_w = gla_norm.shape[1]
    ch = conv_w.shape[2]
    alr_off = 2 * key_w + 2 * val_w
    glu_off = alr_off
    gate_off = glu_off + 2 * ch

    def rows(p):
        return p.reshape(p.shape[0], 1, p.shape[1])

    wa = jnp.pad(w_alpha2, ((0, 0), (0, LANES - rank), (0, 0))).astype(BF16)
    norm_mix3, norm_ffn3 = rows(norm_mix), rows(norm_ffn)
    ba3, gn3 = rows(b_alpha2), rows(gla_norm)
    cb3, lg3, lb3, bpw3 = rows(conv_b), rows(conv_norm_g), rows(conv_norm_b), rows(b_pw2)

    w_in_t = jnp.swapaxes(w_in, 1, 2)
    h = x.reshape(T, D)
    for l in range(L):
        xn, alr = _mix_norm(h, norm_mix3, w_in_t, l, alr_off=alr_off)
        proj = _in_proj(xn, w_in_t, l, alr_off=alr_off, rank=rank)
        proj3 = proj.reshape(B, S, proj.shape[1])
        xa = _gla(alr.reshape(B, S, LANES), proj3, wa, ba3, gn3, l, key_w=key_w, val_w=val_w)
        u = _conv_branch(proj3, conv_w, cb3, lg3, lb3, l, glu_off=glu_off, ch=ch)
        merged = _merge(xa.reshape(T, val_w), u.reshape(T, ch), proj, w_o_gla, w_pw2, bpw3, l,
                        gate_off=gate_off)
        h, xn_ffn = _out_proj(merged, w_out, h, norm_ffn3, l)
        act = _ffn_up(xn_ffn, w_gate, w_up, l)
        h = _ffn_down(act, w_down, h, l)
    out = _norm(h, norm_final.reshape(1, 1, D), 0, F32, "final_norm")
    return out.reshape(B, S, D)
```

```python
import functools

import jax
import jax.numpy as jnp
from jax import lax
from jax.experimental import pallas as pl
from jax.experimental.pallas import tpu as pltpu

F32 = jnp.float32
BF16 = jnp.bfloat16

RMS_EPS = 1e-6
LN_EPS = 1e-5
GLA_HEADS = 4
GLA_CHUNK = 64
GLA_GATE_TAU = 16.0

LANES = 128
SUBLANES = 8
VMEM_LIMIT_BYTES = 56 * 1024 * 1024


def _params(*sem, vmem_limit_bytes=VMEM_LIMIT_BYTES):
    return pltpu.CompilerParams(dimension_semantics=sem, vmem_limit_bytes=vmem_limit_bytes)


def _tile(n, pref):
    t = min(n, pref)
    while n % t:
        t //= 2
    return t


def _dot(a, b):
    return jnp.dot(a, b, preferred_element_type=F32)


def _dot_nt(a, b):
    return lax.dot_general(a, b, (((1,), (1,)), ((), ())), preferred_element_type=F32)


def _dot_tn(a, b):
    return lax.dot_general(a, b, (((0,), (0,)), ((), ())), preferred_element_type=F32)


def _sigmoid(x):
    return 0.5 * jnp.tanh(0.5 * x) + 0.5


def _silu(x):
    h = 0.5 * x
    return h + h * jnp.tanh(h)


def _rmsnorm_rows(h_ref, g_ref, out_ref, row_chunk):
    g = g_ref[...]

    def body(i, carry):
        rows = pl.ds(pl.multiple_of(i * row_chunk, row_chunk), row_chunk)
        x = h_ref[rows, :]
        ms = jnp.mean(x * x, axis=-1, keepdims=True)
        out_ref[rows, :] = (x * lax.rsqrt(ms + RMS_EPS) * g).astype(out_ref.dtype)
        return carry

    lax.fori_loop(0, h_ref.shape[0] // row_chunk, body, 0, unroll=2)


def _cast_rows(w_ref, out_ref, row_chunk):
    def body(i, carry):
        rows = pl.ds(pl.multiple_of(i * row_chunk, row_chunk), row_chunk)
        out_ref[rows, :] = w_ref[rows, :].astype(out_ref.dtype)
        return carry

    lax.fori_loop(0, w_ref.shape[0] // row_chunk, body, 0)


def _vec_spec(n, l):
    return pl.BlockSpec((None, 1, n), lambda *idx: (l, 0, 0))


def _norm_body(h_ref, g_ref, o_ref, *, row_chunk):
    _rmsnorm_rows(h_ref, g_ref, o_ref, row_chunk)


def _norm(h, gains, l, out_dtype, name):
    T, D = h.shape
    tm = _tile(T, 512)
    return pl.pallas_call(
        functools.partial(_norm_body, row_chunk=_tile(tm, 64)),
        grid=(T // tm,),
        in_specs=[pl.BlockSpec((tm, D), lambda i: (i, 0)), _vec_spec(D, l)],
        out_specs=pl.BlockSpec((tm, D), lambda i: (i, 0)),
        out_shape=jax.ShapeDtypeStruct((T, D), out_dtype),
        compiler_params=_params("parallel"),
        name=name,
    )(h, gains)


def _mix_norm_body(h_ref, g_ref, walr_ref, xn_ref, alr_ref, walr_scr, *, row_chunk):
    @pl.when(pl.program_id(0) == 0)
    def _():
        walr_scr[...] = walr_ref[...].astype(BF16)

    _rmsnorm_rows(h_ref, g_ref, xn_ref, row_chunk)
    alr_ref[...] = _dot_nt(xn_ref[...], walr_scr[...])


def _mix_norm(h, gains, w_in_t, l, *, alr_off):
    T, D = h.shape
    tm = _tile(T, 512)
    assert alr_off % LANES == 0
    return pl.pallas_call(
        functools.partial(_mix_norm_body, row_chunk=_tile(tm, 64)),
        grid=(T // tm,),
        in_specs=[
            pl.BlockSpec((tm, D), lambda i: (i, 0)),
            _vec_spec(D, l),
            pl.BlockSpec((None, LANES, D), lambda i: (l, alr_off // LANES, 0)),
        ],
        out_specs=[pl.BlockSpec((tm, D), lambda i: (i, 0)), pl.BlockSpec((tm, LANES), lambda i: (i, 0))],
        out_shape=[jax.ShapeDtypeStruct((T, D), BF16), jax.ShapeDtypeStruct((T, LANES), F32)],
        scratch_shapes=[pltpu.VMEM((LANES, D), BF16)],
        compiler_params=_params("arbitrary"),
        name="mix_norm",
    )(h, gains, w_in_t)


def _in_proj_body(x_ref, wt_ref, o_ref, wscr_ref, *, row_chunk):
    @pl.when(pl.program_id(1) == 0)
    def _():
        _cast_rows(wt_ref.at[0], wscr_ref, row_chunk)

    o_ref[...] = _dot_nt(x_ref[...], wscr_ref[...]).astype(o_ref.dtype)


def _in_proj(xn, w_in_t, l, *, alr_off, rank):
    T, D = xn.shape
    NW = w_in_t.shape[1] - rank
    tm, tn = _tile(T, 2048), _tile(NW, 1024)
    assert alr_off % tn == 0 and rank % SUBLANES == 0
    skip_from = alr_off // tn

    def w_rows(j, i):
        return (l, (j * (tn // SUBLANES) + jnp.where(j >= skip_from, rank // SUBLANES, 0)) * SUBLANES, 0)

    return pl.pallas_call(
        functools.partial(_in_proj_body, row_chunk=_tile(tn, 64)),
        grid=(NW // tn, T // tm),
        in_specs=[
            pl.BlockSpec((tm, D), lambda j, i: (i, 0)),
            pl.BlockSpec((pl.Element(1), pl.Element(tn), pl.Element(D)), w_rows),
        ],
        out_specs=pl.BlockSpec((tm, tn), lambda j, i: (i, j)),
        out_shape=jax.ShapeDtypeStruct((T, NW), BF16),
        scratch_shapes=[pltpu.VMEM((tn, D), BF16)],
        compiler_params=_params("arbitrary", "arbitrary"),
        name="in_proj",
    )(xn, w_in_t)


def _log_sigmoid(z):
    return jnp.minimum(z, 0.0) - jnp.log1p(jnp.exp(-jnp.abs(z)))


def _cumsum_rows(tril, x):
    hi = x.astype(BF16)
    r1 = x - hi.astype(F32)
    mid = r1.astype(BF16)
    lo = (r1 - mid.astype(F32)).astype(BF16)
    return _dot(tril, hi) + _dot(tril, mid) + _dot(tril, lo)


def _gla_body(alr_ref, q_ref, k_ref, v_ref, r_ref, wa_ref, ba_ref, gn_ref, o_ref,
              state_ref, att_ref, *, n_chunks, n_heads, scale):
    C = GLA_CHUNK
    NC = n_chunks
    _, DK, DV = state_ref.shape
    TS = NC * C
    heads = range(n_heads)
    kcs = [slice(hh * DK, (hh + 1) * DK) for hh in heads]
    vcs = [slice(hh * DV, (hh + 1) * DV) for hh in heads]
    chunk = [slice(c * C, (c + 1) * C) for c in range(NC)]

    @pl.when(pl.program_id(2) == 0)
    def _():
        state_ref[...] = jnp.zeros_like(state_ref)

    row = lax.broadcasted_iota(jnp.int32, (TS, TS), 0)
    col = lax.broadcasted_iota(jnp.int32, (TS, TS), 1)
    chunk_shift = C.bit_length() - 1
    causal = (row >= col) & ((row >> chunk_shift) == (col >> chunk_shift))
    tril = causal.astype(BF16)

    z = _dot(alr_ref[...].astype(BF16), wa_ref[...]) + ba_ref[...]
    log_a = _log_sigmoid(z) / GLA_GATE_TAU
    b = _cumsum_rows(tril, log_a)
    bl = [b[(c + 1) * C - 1:(c + 1) * C, :] for c in range(NC)]
    E = [jnp.zeros_like(bl[0])]
    for c in range(NC):
        E.append(E[c] + bl[c])
    bl_rows = jnp.concatenate([jnp.broadcast_to(x, (C, x.shape[1])) for x in bl], axis=0)
    qf = q_ref[...].astype(F32) * scale * jnp.exp(b)
    kf = k_ref[...].astype(F32)
    kdf = kf * jnp.exp(bl_rows - b)
    qe = qf.astype(BF16)
    ke = (kf * jnp.exp(-b)).astype(BF16)
    kd = kdf.astype(BF16)
    cross_lhs = []
    for j in range(NC - 1):
        parts = [qe[chunk[j + 1], :]]
        parts += [(qf[chunk[c], :] * jnp.exp(E[c] - E[j + 1])).astype(BF16) for c in range(j + 2, NC)]
        cross_lhs.append(jnp.concatenate(parts, axis=0))
    qs = jnp.concatenate(
        [qe[chunk[0], :]] + [(qf[chunk[c], :] * jnp.exp(E[c])).astype(BF16) for c in range(1, NC)], axis=0)
    kds = jnp.concatenate(
        [(kdf[chunk[c], :] * jnp.exp(E[NC] - E[c + 1])).astype(BF16) for c in range(NC - 1)]
        + [kd[chunk[NC - 1], :]], axis=0)
    g_end = jnp.exp(E[NC])

    for hh in heads:
        att_ref[hh] = jnp.where(causal, _dot_nt(qe[:, kcs[hh]], ke[:, kcs[hh]]), 0.0)
    for j in range(NC - 1):
        for hh in heads:
            att_ref[hh, (j + 1) * C:TS, chunk[j]] = _dot_nt(cross_lhs[j][:, kcs[hh]], kd[chunk[j], kcs[hh]])
    o = []
    for hh in heads:
        lhs_o = jnp.concatenate([att_ref[hh].astype(BF16), qs[:, kcs[hh]]], axis=1)
        rhs_o = jnp.concatenate([v_ref[:, vcs[hh]], state_ref[hh].astype(BF16)], axis=0)
        o.append(_dot(lhs_o, rhs_o))
    lane_blk = min(LANES, DV)
    for hh in heads:
        upd = _dot_tn(kds[:, kcs[hh]], v_ref[:, vcs[hh]])
        g_col = jnp.broadcast_to(g_end[:, kcs[hh]], (lane_blk, DK)).T
        for jb in range(DV // lane_blk):
            cols = slice(jb * lane_blk, (jb + 1) * lane_blk)
            state_ref[hh, :, cols] = state_ref[hh, :, cols] * g_col + upd[:, cols]
    for hh in heads:
        ms = jnp.mean(o[hh] * o[hh], axis=-1, keepdims=True)
        on = o[hh] * lax.rsqrt(ms + RMS_EPS) * gn_ref[:, vcs[hh]]
        r = r_ref[:, vcs[hh]].astype(F32)
        o_ref[:, vcs[hh]] = (on * _silu(r)).astype(o_ref.dtype)


def _gla(alr3, proj3, wa, ba, gn, l, *, key_w, val_w):
    B, S, RP = alr3.shape
    H = GLA_HEADS
    DK, DV = key_w // H, val_w // H
    ts = _tile(S, 256)
    hps = H
    KB, VB = hps * DK, hps * DV
    assert ts % GLA_CHUNK == 0 and (2 * key_w) % VB == 0 and RP == wa.shape[1]
    k_blk0 = key_w // KB
    v_blk0 = (2 * key_w) // VB
    r_blk0 = (2 * key_w + val_w) // VB
    return pl.pallas_call(
        functools.partial(_gla_body, n_chunks=ts // GLA_CHUNK, n_heads=hps, scale=DK ** -0.5),
        grid=(B, H // hps, S // ts),
        in_specs=[
            pl.BlockSpec((None, ts, RP), lambda b, h, s: (b, s, 0)),
            pl.BlockSpec((None, ts, KB), lambda b, h, s: (b, s, h)),
            pl.BlockSpec((None, ts, KB), lambda b, h, s: (b, s, k_blk0 + h)),
            pl.BlockSpec((None, ts, VB), lambda b, h, s: (b, s, v_blk0 + h)),
            pl.BlockSpec((None, ts, VB), lambda b, h, s: (b, s, r_blk0 + h)),
            pl.BlockSpec((None, RP, KB), lambda b, h, s: (l, 0, h)),
            pl.BlockSpec((None, 1, KB), lambda b, h, s: (l, 0, h)),
            pl.BlockSpec((None, 1, VB), lambda b, h, s: (l, 0, h)),
        ],
        out_specs=pl.BlockSpec((None, ts, VB), lambda b, h, s: (b, s, h)),
        out_shape=jax.ShapeDtypeStruct((B, S, val_w), BF16),
        scratch_shapes=[pltpu.VMEM((hps, DK, DV), F32), pltpu.VMEM((hps, ts, ts), F32)],
        compiler_params=_params("parallel", "parallel", "arbitrary"),
        name="gla",
    )(alr3, proj3, proj3, proj3, proj3, wa, ba, gn)


_CONV_PITCH = 4


def _conv_body(val_ref, gate_ref, w_ref, cb_ref, lg_ref, lb_ref, o_ref, uext_ref, y_ref,
               *, halo, row_blk, ln_blk):
    ts, CH = val_ref.shape
    W = w_ref.shape[0]
    n_lb, _, lane_blk = uext_ref.shape
    step = pl.program_id(1)

    @pl.when(step == 0)
    def _():
        uext_ref[:, 0:halo, :] = jnp.zeros((n_lb, halo, lane_blk), F32)

    @pl.when(step > 0)
    def _():
        uext_ref[:, 0:halo, :] = uext_ref[:, ts:ts + halo, :]

    u = val_ref[...].astype(F32) * _sigmoid(gate_ref[...].astype(F32))
    for lb in range(n_lb):
        uext_ref[lb, halo:halo + ts, :] = u[:, lb * lane_blk:(lb + 1) * lane_blk]

    first = halo - (W - 1)
    span = _CONV_PITCH * SUBLANES

    def conv_lanes(lb, carry):
        lanes = pl.ds(pl.multiple_of(lb * lane_blk, lane_blk), lane_blk)
        taps = [w_ref[j:j + 1, lanes] for j in range(W)]
        bias = jnp.broadcast_to(cb_ref[:, lanes], (SUBLANES, lane_blk))
        for rb in range(ts // row_blk):
            starts = [rb * row_blk + g * span + p for g in range(row_blk // span) for p in range(_CONV_PITCH)]
            accs = [bias for _ in starts]
            for j in range(W):
                for a, st in enumerate(starts):
                    src = uext_ref[lb, pl.ds(first + j + st, SUBLANES, stride=_CONV_PITCH), :]
                    accs[a] = accs[a] + taps[j] * src
            for a, st in enumerate(starts):
                y_ref[lb, pl.ds(st, SUBLANES, stride=_CONV_PITCH), :] = accs[a]
        return carry

    lax.fori_loop(0, n_lb, conv_lanes, 0)

    lg = lg_ref[...]
    lb_ = lb_ref[...]

    def ln_rows(i, carry):
        rows = pl.ds(pl.multiple_of(i * ln_blk, ln_blk), ln_blk)
        x = jnp.concatenate([y_ref[lb, rows, :] for lb in range(n_lb)], axis=1)
        mu = jnp.mean(x, axis=-1, keepdims=True)
        xc = x - mu
        var = jnp.mean(xc * xc, axis=-1, keepdims=True)
        y = xc * lax.rsqrt(var + LN_EPS) * lg + lb_
        o_ref[rows, :] = _silu(y).astype(o_ref.dtype)
        return carry

    lax.fori_loop(0, ts // ln_blk, ln_rows, 0, unroll=True)


def _conv_branch(proj3, conv_w, conv_b, ln_g, ln_b, l, *, glu_off, ch):
    B, S, _ = proj3.shape
    W = conv_w.shape[1]
    halo = -(-(W - 1) // SUBLANES) * SUBLANES
    ts = _tile(S, 512)
    lane_blk = min(LANES, ch)
    assert ts >= halo and glu_off % ch == 0 and ts % (_CONV_PITCH * SUBLANES) == 0
    blk0 = glu_off // ch
    return pl.pallas_call(
        functools.partial(_conv_body, halo=halo, row_blk=_CONV_PITCH * SUBLANES, ln_blk=_tile(ts, 16)),
        grid=(B, S // ts),
        in_specs=[
            pl.BlockSpec((None, ts, ch), lambda b, s: (b, s, blk0)),
            pl.BlockSpec((None, ts, ch), lambda b, s: (b, s, blk0 + 1)),
            pl.BlockSpec((None, W, ch), lambda b, s: (l, 0, 0)),
            _vec_spec(ch, l), _vec_spec(ch, l), _vec_spec(ch, l),
        ],
        out_specs=pl.BlockSpec((None, ts, ch), lambda b, s: (b, s, 0)),
        out_shape=jax.ShapeDtypeStruct((B, S, ch), BF16),
        scratch_shapes=[pltpu.VMEM((ch // lane_blk, halo + ts, lane_blk), F32),
                        pltpu.VMEM((ch // lane_blk, ts, lane_blk), F32)],
        compiler_params=_params("parallel", "arbitrary"),
        name="conv_branch",
    )(proj3, proj3, conv_w, conv_b, ln_g, ln_b)


def _merge_body(xa_ref, u_ref, ga_ref, gb_ref, woa_ref, wpw_ref, bpw_ref, o_ref, woa_scr, wpw_scr,
                *, row_chunk):
    @pl.when(pl.program_id(1) == 0)
    def _():
        _cast_rows(woa_ref, woa_scr, row_chunk)
        _cast_rows(wpw_ref, wpw_scr, row_chunk)

    a = _dot(xa_ref[...], woa_scr[...])
    bb = _dot(u_ref[...], wpw_scr[...]) + bpw_ref[...]
    ga = _sigmoid(ga_ref[...].astype(F32))
    gb = _sigmoid(gb_ref[...].astype(F32))
    o_ref[...] = (ga * a + gb * bb).astype(o_ref.dtype)


def _merge(xa, u, proj, w_o_gla, w_pw2, b_pw2, l, *, gate_off):
    T, VW = xa.shape
    CH = u.shape[1]
    D = w_o_gla.shape[2]
    tm, tn = _tile(T, 512), _tile(D, 1024)
    assert gate_off % tn == 0
    ga0 = gate_off // tn
    gb0 = (gate_off + D) // tn
    return pl.pallas_call(
        functools.partial(_merge_body, row_chunk=_tile(min(VW, CH), 256)),
        grid=(D // tn, T // tm),
        in_specs=[
            pl.BlockSpec((tm, VW), lambda j, i: (i, 0)),
            pl.BlockSpec((tm, CH), lambda j, i: (i, 0)),
            pl.BlockSpec((tm, tn), lambda j, i: (i, ga0 + j)),
            pl.BlockSpec((tm, tn), lambda j, i: (i, gb0 + j)),
            pl.BlockSpec((None, VW, tn), lambda j, i: (l, 0, j), pipeline_mode=pl.Buffered(1)),
            pl.BlockSpec((None, CH, tn), lambda j, i: (l, 0, j), pipeline_mode=pl.Buffered(1)),
            pl.BlockSpec((None, 1, tn), lambda j, i: (l, 0, j)),
        ],
        out_specs=pl.BlockSpec((tm, tn), lambda j, i: (i, j)),
        out_shape=jax.ShapeDtypeStruct((T, D), BF16),
        scratch_shapes=[pltpu.VMEM((VW, tn), BF16), pltpu.VMEM((CH, tn), BF16)],
        compiler_params=_params("arbitrary", "arbitrary"),
        name="merge",
    )(xa, u, proj, proj, w_o_gla, w_pw2, b_pw2)


def _out_proj_body(x_ref, w_ref, h_ref, g_ref, ho_ref, xn_ref, wscr_ref, *, row_chunk):
    @pl.when(pl.program_id(0) == 0)
    def _():
        _cast_rows(w_ref, wscr_ref, row_chunk)

    ho_ref[...] = h_ref[...] + _dot(x_ref[...], wscr_ref[...])
    _rmsnorm_rows(ho_ref, g_ref, xn_ref, _tile(ho_ref.shape[0], 64))


def _out_proj(x, w_out, h, gains, l):
    T, K = x.shape
    D = w_out.shape[2]
    tm = _tile(T, 512)
    return pl.pallas_call(
        functools.partial(_out_proj_body, row_chunk=_tile(K, 256)),
        grid=(T // tm,),
        in_specs=[
            pl.BlockSpec((tm, K), lambda i: (i, 0)),
            pl.BlockSpec((None, K, D), lambda i: (l, 0, 0), pipeline_mode=pl.Buffered(1)),
            pl.BlockSpec((tm, D), lambda i: (i, 0)),
            _vec_spec(D, l),
        ],
        out_specs=[pl.BlockSpec((tm, D), lambda i: (i, 0)), pl.BlockSpec((tm, D), lambda i: (i, 0))],
        out_shape=[jax.ShapeDtypeStruct((T, D), F32), jax.ShapeDtypeStruct((T, D), BF16)],
        scratch_shapes=[pltpu.VMEM((K, D), BF16)],
        compiler_params=_params("arbitrary"),
        name="out_proj",
    )(x, w_out, h, gains)


def _ffn_up_body(x_ref, wg_ref, wu_ref, o_ref, wg_scr, wu_scr, *, row_chunk):
    @pl.when(pl.program_id(1) == 0)
    def _():
        _cast_rows(wg_ref, wg_scr, row_chunk)
        _cast_rows(wu_ref, wu_scr, row_chunk)

    xn = x_ref[...]
    gate = _dot(xn, wg_scr[...])
    up = _dot(xn, wu_scr[...])
    o_ref[...] = (_silu(gate) * up).astype(o_ref.dtype)


def _ffn_up(xn, w_gate, w_up, l):
    T, D = xn.shape
    F = w_gate.shape[2]
    tm, tn = _tile(T, 1024), _tile(F, 512)
    return pl.pallas_call(
        functools.partial(_ffn_up_body, row_chunk=_tile(D, 256)),
        grid=(F // tn, T // tm),
        in_specs=[
            pl.BlockSpec((tm, D), lambda j, i: (i, 0)),
            pl.BlockSpec((None, D, tn), lambda j, i: (l, 0, j)),
            pl.BlockSpec((None, D, tn), lambda j, i: (l, 0, j)),
        ],
        out_specs=pl.BlockSpec((tm, tn), lambda j, i: (i, j)),
        out_shape=jax.ShapeDtypeStruct((T, F), BF16),
        scratch_shapes=[pltpu.VMEM((D, tn), BF16), pltpu.VMEM((D, tn), BF16)],
        compiler_params=_params("arbitrary", "arbitrary"),
        name="ffn_up",
    )(xn, w_gate, w_up)


def _ffn_down_body(x_ref, w_ref, h_ref, o_ref, wscr_ref, *, row_chunk):
    @pl.when(pl.program_id(1) == 0)
    def _():
        _cast_rows(w_ref, wscr_ref, row_chunk)

    o_ref[...] = h_ref[...] + _dot(x_ref[...], wscr_ref[...])


def _ffn_down(x, w_down, h, l):
    T, K = x.shape
    D = w_down.shape[2]
    tm, tn = _tile(T, 512), _tile(D, 1024)
    return pl.pallas_call(
        functools.partial(_ffn_down_body, row_chunk=_tile(K, 256)),
        grid=(D // tn, T // tm),
        in_specs=[
            pl.BlockSpec((tm, K), lambda j, i: (i, 0)),
            pl.BlockSpec((None, K, tn), lambda j, i: (l, 0, j), pipeline_mode=pl.Buffered(1)),
            pl.BlockSpec((tm, tn), lambda j, i: (i, j)),
        ],
        out_specs=pl.BlockSpec((tm, tn), lambda j, i: (i, j)),
        out_shape=jax.ShapeDtypeStruct((T, D), F32),
        scratch_shapes=[pltpu.VMEM((K, tn), BF16)],
        compiler_params=_params("arbitrary", "arbitrary"),
        name="ffn_down",
    )(x, w_down, h)


def kernel(x, norm_mix, w_in, w_alpha2, b_alpha2, gla_norm, w_o_gla, conv_w, conv_b, conv_norm_g,
           conv_norm_b, w_pw2, b_pw2, w_out, norm_ffn, w_gate, w_up, w_down, norm_final):
    B, S, D = x.shape
    L = w_in.shape[0]
    T = B * S
    rank, key_w = w_alpha2.shape[1], w_alpha2.shape[2]
    val_w = gla_norm.shape[1]
    ch = conv_w.shape[2]
    alr_off = 2 * key_w + 2 * val_w
    glu_off = alr_off
    gate_off = glu_off + 2 * ch

    def rows(p):
        return p.reshape(p.shape[0], 1, p.shape[1])

    wa = jnp.pad(w_alpha2, ((0, 0), (0, LANES - rank), (0, 0))).astype(BF16)
    norm_mix3, norm_ffn3 = rows(norm_mix), rows(norm_ffn)
    ba3, gn3 = rows(b_alpha2), rows(gla_norm)
    cb3, lg3, lb3, bpw3 = rows(conv_b), rows(conv_norm_g), rows(conv_norm_b), rows(b_pw2)

    w_in_t = jnp.swapaxes(w_in, 1, 2)
    h = x.reshape(T, D)
    for l in range(L):
        xn, alr = _mix_norm(h, norm_mix3, w_in_t, l, alr_off=alr_off)
        proj = _in_proj(xn, w_in_t, l, alr_off=alr_off, rank=rank)
        proj3 = proj.reshape(B, S, proj.shape[1])
        xa = _gla(alr.reshape(B, S, LANES), proj3, wa, ba3, gn3, l, key_w=key_w, val_w=val_w)
        u = _conv_branch(proj3, conv_w, cb3, lg3, lb3, l, glu_off=glu_off, ch=ch)
        merged = _merge(xa.reshape(T, val_w), u.reshape(T, ch), proj, w_o_gla, w_pw2, bpw3, l,
                        gate_off=gate_off)
        h, xn_ffn = _out_proj(merged, w_out, h, norm_ffn3, l)
        act = _ffn_up(xn_ffn, w_gate, w_up, l)
        h = _ffn_down(act, w_down, h, l)
    out = _norm(h, norm_final.reshape(1, 1, D), 0, F32, "final_norm")
    return out.reshape(B, S, D)
```

```python
import functools

import jax
import jax.numpy as jnp
from jax import lax
from jax.experimental import pallas as pl
from jax.experimental.pallas import tpu as pltpu

F32 = jnp.float32
BF16 = jnp.bfloat16

RMS_EPS = 1e-6
LN_EPS = 1e-5
GLA_HEADS = 4
GLA_CHUNK = 64
GLA_GATE_TAU = 16.0

LANES = 128
SUBLANES = 8
VMEM_LIMIT_BYTES = 56 * 1024 * 1024


def _params(*sem, vmem_limit_bytes=VMEM_LIMIT_BYTES):
    return pltpu.CompilerParams(dimension_semantics=sem, vmem_limit_bytes=vmem_limit_bytes)


def _tile(n, pref):
    t = min(n, pref)
    while n % t:
        t //= 2
    return t


def _dot(a, b):
    return jnp.dot(a, b, preferred_element_type=F32)


def _dot_nt(a, b):
    return lax.dot_general(a, b, (((1,), (1,)), ((), ())), preferred_element_type=F32)


def _dot_tn(a, b):
    return lax.dot_general(a, b, (((0,), (0,)), ((), ())), preferred_element_type=F32)


def _sigmoid(x):
    return 0.5 * jnp.tanh(0.5 * x) + 0.5


def _silu(x):
    h = 0.5 * x
    return h + h * jnp.tanh(h)


def _rmsnorm_rows(h_ref, g_ref, out_ref, row_chunk):
    g = g_ref[...]

    def body(i, carry):
        rows = pl.ds(pl.multiple_of(i * row_chunk, row_chunk), row_chunk)
        x = h_ref[rows, :]
        ms = jnp.mean(x * x, axis=-1, keepdims=True)
        out_ref[rows, :] = (x * lax.rsqrt(ms + RMS_EPS) * g).astype(out_ref.dtype)
        return carry

    lax.fori_loop(0, h_ref.shape[0] // row_chunk, body, 0, unroll=2)


def _cast_rows(w_ref, out_ref, row_chunk):
    def body(i, carry):
        rows = pl.ds(pl.multiple_of(i * row_chunk, row_chunk), row_chunk)
        out_ref[rows, :] = w_ref[rows, :].astype(out_ref.dtype)
        return carry

    lax.fori_loop(0, w_ref.shape[0] // row_chunk, body, 0)


def _vec_spec(n, l):
    return pl.BlockSpec((None, 1, n), lambda *idx: (l, 0, 0))


def _norm_body(h_ref, g_ref, o_ref, *, row_chunk):
    _rmsnorm_rows(h_ref, g_ref, o_ref, row_chunk)


def _norm(h, gains, l, out_dtype, name):
    T, D = h.shape
    tm = _tile(T, 512)
    return pl.pallas_call(
        functools.partial(_norm_body, row_chunk=_tile(tm, 64)),
        grid=(T // tm,),
        in_specs=[pl.BlockSpec((tm, D), lambda i: (i, 0)), _vec_spec(D, l)],
        out_specs=pl.BlockSpec((tm, D), lambda i: (i, 0)),
        out_shape=jax.ShapeDtypeStruct((T, D), out_dtype),
        compiler_params=_params("parallel"),
        name=name,
    )(h, gains)


def _mix_norm_body(h_ref, g_ref, walr_ref, xn_ref, alr_ref, walr_scr, *, row_chunk):
    @pl.when(pl.program_id(0) == 0)
    def _():
        walr_scr[...] = walr_ref[...].astype(BF16)

    _rmsnorm_rows(h_ref, g_ref, xn_ref, row_chunk)
    alr_ref[...] = _dot_nt(xn_ref[...], walr_scr[...])


def _mix_norm(h, gains, w_in_t, l, *, alr_off):
    T, D = h.shape
    tm = _tile(T, 512)
    assert alr_off % LANES == 0
    return pl.pallas_call(
        functools.partial(_mix_norm_body, row_chunk=_tile(tm, 64)),
        grid=(T // tm,),
        in_specs=[
            pl.BlockSpec((tm, D), lambda i: (i, 0)),
            _vec_spec(D, l),
            pl.BlockSpec((None, LANES, D), lambda i: (l, alr_off // LANES, 0)),
        ],
        out_specs=[pl.BlockSpec((tm, D), lambda i: (i, 0)), pl.BlockSpec((tm, LANES), lambda i: (i, 0))],
        out_shape=[jax.ShapeDtypeStruct((T, D), BF16), jax.ShapeDtypeStruct((T, LANES), F32)],
        scratch_shapes=[pltpu.VMEM((LANES, D), BF16)],
        compiler_params=_params("arbitrary"),
        name="mix_norm",
    )(h, gains, w_in_t)


def _in_proj_body(x_ref, wt_ref, o_ref, wscr_ref, *, row_chunk):
    @pl.when(pl.program_id(1) == 0)
    def _():
        _cast_rows(wt_ref.at[0], wscr_ref, row_chunk)

    o_ref[...] = _dot_nt(x_ref[...], wscr_ref[...]).astype(o_ref.dtype)


def _in_proj(xn, w_in_t, l, *, alr_off, rank):
    T, D = xn.shape
    NW = w_in_t.shape[1] - rank
    tm, tn = _tile(T, 2048), _tile(NW, 1024)
    assert alr_off % tn == 0 and rank % SUBLANES == 0
    skip_from = alr_off // tn

    def w_rows(j, i):
        return (l, (j * (tn // SUBLANES) + jnp.where(j >= skip_from, rank // SUBLANES, 0)) * SUBLANES, 0)

    return pl.pallas_call(
        functools.partial(_in_proj_body, row_chunk=_tile(tn, 64)),
        grid=(NW // tn, T // tm),
        in_specs=[
            pl.BlockSpec((tm, D), lambda j, i: (i, 0)),
            pl.BlockSpec((pl.Element(1), pl.Element(tn), pl.Element(D)), w_rows),
        ],
        out_specs=pl.BlockSpec((tm, tn), lambda j, i: (i, j)),
        out_shape=jax.ShapeDtypeStruct((T, NW), BF16),
        scratch_shapes=[pltpu.VMEM((tn, D), BF16)],
        compiler_params=_params("arbitrary", "arbitrary"),
        name="in_proj",
    )(xn, w_in_t)


def _log_sigmoid(z):
    return jnp.minimum(z, 0.0) - jnp.log1p(jnp.exp(-jnp.abs(z)))


def _cumsum_rows(tril, x):
    hi = x.astype(BF16)
    r1 = x - hi.astype(F32)
    mid = r1.astype(BF16)
    lo = (r1 - mid.astype(F32)).astype(BF16)
    return _dot(tril, hi) + _dot(tril, mid) + _dot(tril, lo)


def _gla_body(alr_ref, q_ref, k_ref, v_ref, r_ref, wa_ref, ba_ref, gn_ref, o_ref,
              state_ref, att_ref, *, n_chunks, n_heads, scale):
    C = GLA_CHUNK
    NC = n_chunks
    _, DK, DV = state_ref.shape
    TS = NC * C
    heads = range(n_heads)
    kcs = [slice(hh * DK, (hh + 1) * DK) for hh in heads]
    vcs = [slice(hh * DV, (hh + 1) * DV) for hh in heads]
    chunk = [slice(c * C, (c + 1) * C) for c in range(NC)]

    @pl.when(pl.program_id(2) == 0)
    def _():
        state_ref[...] = jnp.zeros_like(state_ref)

    row = lax.broadcasted_iota(jnp.int32, (TS, TS), 0)
    col = lax.broadcasted_iota(jnp.int32, (TS, TS), 1)
    chunk_shift = C.bit_length() - 1
    causal = (row >= col) & ((row >> chunk_shift) == (col >> chunk_shift))
    tril = causal.astype(BF16)

    z = _dot(alr_ref[...].astype(BF16), wa_ref[...]) + ba_ref[...]
    log_a = _log_sigmoid(z) / GLA_GATE_TAU
    b = _cumsum_rows(tril, log_a)
    bl = [b[(c + 1) * C - 1:(c + 1) * C, :] for c in range(NC)]
    E = [jnp.zeros_like(bl[0])]
    for c in range(NC):
        E.append(E[c] + bl[c])
    bl_rows = jnp.concatenate([jnp.broadcast_to(x, (C, x.shape[1])) for x in bl], axis=0)
    qf = q_ref[...].astype(F32) * scale * jnp.exp(b)
    kf = k_ref[...].astype(F32)
    kdf = kf * jnp.exp(bl_rows - b)
    qe = qf.astype(BF16)
    ke = (kf * jnp.exp(-b)).astype(BF16)
    kd = kdf.astype(BF16)
    cross_lhs = []
    for j in range(NC - 1):
        parts = [qe[chunk[j + 1], :]]
        parts += [(qf[chunk[c], :] * jnp.exp(E[c] - E[j + 1])).astype(BF16) for c in range(j + 2, NC)]
        cross_lhs.append(jnp.concatenate(parts, axis=0))
    qs = jnp.concatenate(
        [qe[chunk[0], :]] + [(qf[chunk[c], :] * jnp.exp(E[c])).astype(BF16) for c in range(1, NC)], axis=0)
    kds = jnp.concatenate(
        [(kdf[chunk[c], :] * jnp.exp(E[NC] - E[c + 1])).astype(BF16) for c in range(NC - 1)]
        + [kd[chunk[NC - 1], :]], axis=0)
    g_end = jnp.exp(E[NC])

    for hh in heads:
        att_ref[hh] = jnp.where(causal, _dot_nt(qe[:, kcs[hh]], ke[:, kcs[hh]]), 0.0)
    for j in range(NC - 1):
        for hh in heads:
            att_ref[hh, (j + 1) * C:TS, chunk[j]] = _dot_nt(cross_lhs[j][:, kcs[hh]], kd[chunk[j], kcs[hh]])
    o = []
    for hh in heads:
        lhs_o = jnp.concatenate([att_ref[hh].astype(BF16), qs[:, kcs[hh]]], axis=1)
        rhs_o = jnp.concatenate([v_ref[:, vcs[hh]], state_ref[hh].astype(BF16)], axis=0)
        o.append(_dot(lhs_o, rhs_o))
    lane_blk = min(LANES, DV)
    for hh in heads:
        upd = _dot_tn(kds[:, kcs[hh]], v_ref[:, vcs[hh]])
        g_col = jnp.broadcast_to(g_end[:, kcs[hh]], (lane_blk, DK)).T
        for jb in range(DV // lane_blk):
            cols = slice(jb * lane_blk, (jb + 1) * lane_blk)
            state_ref[hh, :, cols] = state_ref[hh, :, cols] * g_col + upd[:, cols]
    for hh in heads:
        ms = jnp.mean(o[hh] * o[hh], axis=-1, keepdims=True)
        on = o[hh] * lax.rsqrt(ms + RMS_EPS) * gn_ref[:, vcs[hh]]
        r = r_ref[:, vcs[hh]].astype(F32)
        o_ref[:, vcs[hh]] = (on * _silu(r)).astype(o_ref.dtype)


def _gla(alr3, proj3, wa, ba, gn, l, *, key_w, val_w):
    B, S, RP = alr3.shape
    H = GLA_HEADS
    DK, DV = key_w // H, val_w // H
    ts = _tile(S, 256)
    hps = H
    KB, VB = hps * DK, hps * DV
    assert ts % GLA_CHUNK == 0 and (2 * key_w) % VB == 0 and RP == wa.shape[1]
    k_blk0 = key_w // KB
    v_blk0 = (2 * key_w) // VB
    r_blk0 = (2 * key_w + val_w) // VB
    return pl.pallas_call(
        functools.partial(_gla_body, n_chunks=ts // GLA_CHUNK, n_heads=hps, scale=DK ** -0.5),
        grid=(B, H // hps, S // ts),
        in_specs=[
            pl.BlockSpec((None, ts, RP), lambda b, h, s: (b, s, 0)),
            pl.BlockSpec((None, ts, KB), lambda b, h, s: (b, s, h)),
            pl.BlockSpec((None, ts, KB), lambda b, h, s: (b, s, k_blk0 + h)),
            pl.BlockSpec((None, ts, VB), lambda b, h, s: (b, s, v_blk0 + h)),
            pl.BlockSpec((None, ts, VB), lambda b, h, s: (b, s, r_blk0 + h)),
            pl.BlockSpec((None, RP, KB), lambda b, h, s: (l, 0, h)),
            pl.BlockSpec((None, 1, KB), lambda b, h, s: (l, 0, h)),
            pl.BlockSpec((None, 1, VB), lambda b, h, s: (l, 0, h)),
        ],
        out_specs=pl.BlockSpec((None, ts, VB), lambda b, h, s: (b, s, h)),
        out_shape=jax.ShapeDtypeStruct((B, S, val_w), BF16),
        scratch_shapes=[pltpu.VMEM((hps, DK, DV), F32), pltpu.VMEM((hps, ts, ts), F32)],
        compiler_params=_params("parallel", "parallel", "arbitrary"),
        name="gla",
    )(alr3, proj3, proj3, proj3, proj3, wa, ba, gn)


_CONV_PITCH = 4


def _conv_body(val_ref, gate_ref, w_ref, cb_ref, lg_ref, lb_ref, o_ref, uext_ref, y_ref,
               *, halo, row_blk, ln_blk):
    ts, CH = val_ref.shape
    W = w_ref.shape[0]
    n_lb, _, lane_blk = uext_ref.shape
    step = pl.program_id(1)

    @pl.when(step == 0)
    def _():
        uext_ref[:, 0:halo, :] = jnp.zeros((n_lb, halo, lane_blk), F32)

    @pl.when(step > 0)
    def _():
        uext_ref[:, 0:halo, :] = uext_ref[:, ts:ts + halo, :]

    u = val_ref[...].astype(F32) * _sigmoid(gate_ref[...].astype(F32))
    for lb in range(n_lb):
        uext_ref[lb, halo:halo + ts, :] = u[:, lb * lane_blk:(lb + 1) * lane_blk]

    first = halo - (W - 1)
    span = _CONV_PITCH * SUBLANES

    def conv_lanes(lb, carry):
        lanes = pl.ds(pl.multiple_of(lb * lane_blk, lane_blk), lane_blk)
        taps = [w_ref[j:j + 1, lanes] for j in range(W)]
        bias = jnp.broadcast_to(cb_ref[:, lanes], (SUBLANES, lane_blk))
        for rb in range(ts // row_blk):
            starts = [rb * row_blk + g * span + p for g in range(row_blk // span) for p in range(_CONV_PITCH)]
            accs = [bias for _ in starts]
            for j in range(W):
                for a, st in enumerate(starts):
                    src = uext_ref[lb, pl.ds(first + j + st, SUBLANES, stride=_CONV_PITCH), :]
                    accs[a] = accs[a] + taps[j] * src
            for a, st in enumerate(starts):
                y_ref[lb, pl.ds(st, SUBLANES, stride=_CONV_PITCH), :] = accs[a]
        return carry

    lax.fori_loop(0, n_lb, conv_lanes, 0)

    lg = lg_ref[...]
    lb_ = lb_ref[...]

    def ln_rows(i, carry):
        rows = pl.ds(pl.multiple_of(i * ln_blk, ln_blk), ln_blk)
        x = jnp.concatenate([y_ref[lb, rows, :] for lb in range(n_lb)], axis=1)
        mu = jnp.mean(x, axis=-1, keepdims=True)
        xc = x - mu
        var = jnp.mean(xc * xc, axis=-1, keepdims=True)
        y = xc * lax.rsqrt(var + LN_EPS) * lg + lb_
        o_ref[rows, :] = _silu(y).astype(o_ref.dtype)
        return carry

    lax.fori_loop(0, ts // ln_blk, ln_rows, 0, unroll=True)


def _conv_branch(proj3, conv_w, conv_b, ln_g, ln_b, l, *, glu_off, ch):
    B, S, _ = proj3.shape
    W = conv_w.shape[1]
    halo = -(-(W - 1) // SUBLANES) * SUBLANES
    ts = _tile(S, 512)
    lane_blk = min(LANES, ch)
    assert ts >= halo and glu_off % ch == 0 and ts % (_CONV_PITCH * SUBLANES) == 0
    blk0 = glu_off // ch
    return pl.pallas_call(
        functools.partial(_conv_body, halo=halo, row_blk=_CONV_PITCH * SUBLANES, ln_blk=_tile(ts, 16)),
        grid=(B, S // ts),
        in_specs=[
            pl.BlockSpec((None, ts, ch), lambda b, s: (b, s, blk0)),
            pl.BlockSpec((None, ts, ch), lambda b, s: (b, s, blk0 + 1)),
            pl.BlockSpec((None, W, ch), lambda b, s: (l, 0, 0)),
            _vec_spec(ch, l), _vec_spec(ch, l), _vec_spec(ch, l),
        ],
        out_specs=pl.BlockSpec((None, ts, ch), lambda b, s: (b, s, 0)),
        out_shape=jax.ShapeDtypeStruct((B, S, ch), BF16),
        scratch_shapes=[pltpu.VMEM((ch // lane_blk, halo + ts, lane_blk), F32),
                        pltpu.VMEM((ch // lane_blk, ts, lane_blk), F32)],
        compiler_params=_params("parallel", "arbitrary"),
        name="conv_branch",
    )(proj3, proj3, conv_w, conv_b, ln_g, ln_b)


def _merge_body(xa_ref, u_ref, ga_ref, gb_ref, woa_ref, wpw_ref, bpw_ref, o_ref, woa_scr, wpw_scr,
                *, row_chunk):
    @pl.when(pl.program_id(1) == 0)
    def _():
        _cast_rows(woa_ref, woa_scr, row_chunk)
        _cast_rows(wpw_ref, wpw_scr, row_chunk)

    a = _dot(xa_ref[...], woa_scr[...])
    bb = _dot(u_ref[...], wpw_scr[...]) + bpw_ref[...]
    ga = _sigmoid(ga_ref[...].astype(F32))
    gb = _sigmoid(gb_ref[...].astype(F32))
    o_ref[...] = (ga * a + gb * bb).astype(o_ref.dtype)


def _merge(xa, u, proj, w_o_gla, w_pw2, b_pw2, l, *, gate_off):
    T, VW = xa.shape
    CH = u.shape[1]
    D = w_o_gla.shape[2]
    tm, tn = _tile(T, 512), _tile(D, 1024)
    assert gate_off % tn == 0
    ga0 = gate_off // tn
    gb0 = (gate_off + D) // tn
    return pl.pallas_call(
        functools.partial(_merge_body, row_chunk=_tile(min(VW, CH), 256)),
        grid=(D // tn, T // tm),
        in_specs=[
            pl.BlockSpec((tm, VW), lambda j, i: (i, 0)),
            pl.BlockSpec((tm, CH), lambda j, i: (i, 0)),
            pl.BlockSpec((tm, tn), lambda j, i: (i, ga0 + j)),
            pl.BlockSpec((tm, tn), lambda j, i: (i, gb0 + j)),
            pl.BlockSpec((None, VW, tn), lambda j, i: (l, 0, j), pipeline_mode=pl.Buffered(1)),
            pl.BlockSpec((None, CH, tn), lambda j, i: (l, 0, j), pipeline_mode=pl.Buffered(1)),
            pl.BlockSpec((None, 1, tn), lambda j, i: (l, 0, j)),
        ],
        out_specs=pl.BlockSpec((tm, tn), lambda j, i: (i, j)),
        out_shape=jax.ShapeDtypeStruct((T, D), BF16),
        scratch_shapes=[pltpu.VMEM((VW, tn), BF16), pltpu.VMEM((CH, tn), BF16)],
        compiler_params=_params("arbitrary", "arbitrary"),
        name="merge",
    )(xa, u, proj, proj, w_o_gla, w_pw2, b_pw2)


def _out_proj_body(x_ref, w_ref, h_ref, g_ref, ho_ref, xn_ref, wscr_ref, *, row_chunk):
    @pl.when(pl.program_id(0) == 0)
    def _():
        _cast_rows(w_ref, wscr_ref, row_chunk)

    ho_ref[...] = h_ref[...] + _dot(x_ref[...], wscr_ref[...])
    _rmsnorm_rows(ho_ref, g_ref, xn_ref, _tile(ho_ref.shape[0], 64))


def _out_proj(x, w_out, h, gains, l):
    T, K = x.shape
    D = w_out.shape[2]
    tm = _tile(T, 512)
    return pl.pallas_call(
        functools.partial(_out_proj_body, row_chunk=_tile(K, 256)),
        grid=(T // tm,),
        in_specs=[
            pl.BlockSpec((tm, K), lambda i: (i, 0)),
            pl.BlockSpec((None, K, D), lambda i: (l, 0, 0), pipeline_mode=pl.Buffered(1)),
            pl.BlockSpec((tm, D), lambda i: (i, 0)),
            _vec_spec(D, l),
        ],
        out_specs=[pl.BlockSpec((tm, D), lambda i: (i, 0)), pl.BlockSpec((tm, D), lambda i: (i, 0))],
        out_shape=[jax.ShapeDtypeStruct((T, D), F32), jax.ShapeDtypeStruct((T, D), BF16)],
        scratch_shapes=[pltpu.VMEM((K, D), BF16)],
        compiler_params=_params("arbitrary"),
        name="out_proj",
    )(x, w_out, h, gains)


def _ffn_up_body(x_ref, wg_ref, wu_ref, o_ref, wgu_scr, *, row_chunk):
    tn = o_ref.shape[1]

    @pl.when(pl.program_id(1) == 0)
    def _():
        _cast_rows(wg_ref, wgu_scr.at[:, 0:tn], row_chunk)
        _cast_rows(wu_ref, wgu_scr.at[:, tn:2 * tn], row_chunk)

    gu = _dot(x_ref[...], wgu_scr[...])
    o_ref[...] = (_silu(gu[:, :tn]) * gu[:, tn:]).astype(o_ref.dtype)


def _ffn_up(xn, w_gate, w_up, l):
    T, D = xn.shape
    F = w_gate.shape[2]
    tm, tn = _tile(T, 1024), _tile(F, 512)
    return pl.pallas_call(
        functools.partial(_ffn_up_body, row_chunk=_tile(D, 256)),
        grid=(F // tn, T // tm),
        in_specs=[
            pl.BlockSpec((tm, D), lambda j, i: (i, 0)),
            pl.BlockSpec((None, D, tn), lambda j, i: (l, 0, j)),
            pl.BlockSpec((None, D, tn), lambda j, i: (l, 0, j)),
        ],
        out_specs=pl.BlockSpec((tm, tn), lambda j, i: (i, j)),
        out_shape=jax.ShapeDtypeStruct((T, F), BF16),
        scratch_shapes=[pltpu.VMEM((D, 2 * tn), BF16)],
        compiler_params=_params("arbitrary", "arbitrary"),
        name="ffn_up",
    )(xn, w_gate, w_up)


def _ffn_down_body(x_ref, w_ref, h_ref, o_ref, wscr_ref, *, row_chunk):
    @pl.when(pl.program_id(1) == 0)
    def _():
        _cast_rows(w_ref, wscr_ref, row_chunk)

    o_ref[...] = h_ref[...] + _dot(x_ref[...], wscr_ref[...])


def _ffn_down(x, w_down, h, l):
    T, K = x.shape
    D = w_down.shape[2]
    tm, tn = _tile(T, 512), _tile(D, 1024)
    return pl.pallas_call(
        functools.partial(_ffn_down_body, row_chunk=_tile(K, 256)),
        grid=(D // tn, T // tm),
        in_specs=[
            pl.BlockSpec((tm, K), lambda j, i: (i, 0)),
            pl.BlockSpec((None, K, tn), lambda j, i: (l, 0, j), pipeline_mode=pl.Buffered(1)),
            pl.BlockSpec((tm, tn), lambda j, i: (i, j)),
        ],
        out_specs=pl.BlockSpec((tm, tn), lambda j, i: (i, j)),
        out_shape=jax.ShapeDtypeStruct((T, D), F32),
        scratch_shapes=[pltpu.VMEM((K, tn), BF16)],
        compiler_params=_params("arbitrary", "arbitrary"),
        name="ffn_down",
    )(x, w_down, h)


def kernel(x, norm_mix, w_in, w_alpha2, b_alpha2, gla_norm, w_o_gla, conv_w, conv_b, conv_norm_g,
           conv_norm_b, w_pw2, b_pw2, w_out, norm_ffn, w_gate, w_up, w_down, norm_final):
    B, S, D = x.shape
    L = w_in.shape[0]
    T = B * S
    rank, key_w = w_alpha2.shape[1], w_alpha2.shape[2]
    val_w = gla_norm.shape[1]
    ch = conv_w.shape[2]
    alr_off = 2 * key_w + 2 * val_w
    glu_off = alr_off
    gate_off = glu_off + 2 * ch

    def rows(p):
        return p.reshape(p.shape[0], 1, p.shape[1])

    wa = jnp.pad(w_alpha2, ((0, 0), (0, LANES - rank), (0, 0))).astype(BF16)
    norm_mix3, norm_ffn3 = rows(norm_mix), rows(norm_ffn)
    ba3, gn3 = rows(b_alpha2), rows(gla_norm)
    cb3, lg3, lb3, bpw3 = rows(conv_b), rows(conv_norm_g), rows(conv_norm_b), rows(b_pw2)

    w_in_t = jnp.swapaxes(w_in, 1, 2)
    h = x.reshape(T, D)
    for l in range(L):
        xn, alr = _mix_norm(h, norm_mix3, w_in_t, l, alr_off=alr_off)
        proj = _in_proj(xn, w_in_t, l, alr_off=alr_off, rank=rank)
        proj3 = proj.reshape(B, S, proj.shape[1])
        xa = _gla(alr.reshape(B, S, LANES), proj3, wa, ba3, gn3, l, key_w=key_w, val_w=val_w)
        u = _conv_branch(proj3, conv_w, cb3, lg3, lb3, l, glu_off=glu_off, ch=ch)
        merged = _merge(xa.reshape(T, val_w), u.reshape(T, ch), proj, w_o_gla, w_pw2, bpw3, l,
                        gate_off=gate_off)
        h, xn_ffn = _out_proj(merged, w_out, h, norm_ffn3, l)
        act = _ffn_up(xn_ffn, w_gate, w_up, l)
        h = _ffn_down(act, w_down, h, l)
    out = _norm(h, norm_final.reshape(1, 1, D), 0, F32, "final_norm")
    return out.reshape(B, S, D)
```
